```python
import math
import jax, jax.numpy as jnp
from jax import lax
import numpy as np

D_MODEL = 1024
BATCH = 16
SEQ = 2048
DEPTH = 2
DEC_BATCH = 8
DEC_SEQ = 2048
PAST_LEN = 128

GRID_W = 64
D_A = D_MODEL // 2
HEAD_A = 64
H_A = D_A // HEAD_A
LORA_W = 64
LORA_A = 64
GN_EPS = 6.4e-4
D_B = D_MODEL // 2
HEAD_B = 64
H_B = D_B // HEAD_B
MAX_KR = 8
WIN_C = 16
Q_BLK_W = 16
K_BLK_W = 32
N_CB = GRID_W // Q_BLK_W
RMS_EPS = 1e-6

A_SHIFT = 3 * D_A + 2 * LORA_W + 2 * LORA_A
OFF_GA = A_SHIFT
OFF_QB = OFF_GA + D_A
OFF_KB = OFF_QB + D_B
OFF_VB = OFF_KB + D_B
OFF_GB = OFF_VB + D_B
OFF_MA = OFF_GB + D_B
OFF_MB = OFF_MA + D_MODEL
D_IN = OFF_MB + D_MODEL

kernel_name = "hybrid_rwkv7_natten_encoder"


def rms_norm(x, g):
    xf = x.astype(jnp.float32)
    y = xf * lax.rsqrt(jnp.mean(xf * xf, axis=-1, keepdims=True) + RMS_EPS)
    return (y * g.astype(jnp.float32)).astype(x.dtype)


def centred_shift(z, mu):
    zp = jnp.pad(z[:, :-1], ((0, 0), (1, 0), (0, 0)))
    zn = jnp.pad(z[:, 1:], ((0, 0), (0, 1), (0, 0)))
    return z + mu * (0.5 * (zp + zn) - z)


def _dirs_shared(x):
    return jnp.stack([x, jnp.flip(x, 1)], 0).transpose(2, 0, 1, 3, 4)


def _dirs(x):
    return jnp.stack([x[:, :, 0], jnp.flip(x[:, :, 1], 1)], 0).transpose(2, 0, 1, 3, 4)


def _rwkv7_step(S, inp):
    r, w, k, v, aa, bb = inp
    sa = jnp.einsum('dbhvk,dbhk->dbhv', S, aa)
    S = S * w[..., None, :] + sa[..., None] * bb[..., None, :] + v[..., None] * k[..., None, :]
    o = jnp.einsum('dbhvk,dbhk->dbhv', S, r)
    return S, o


def rwkv7_bidir(r, k, v, wlo, alo, w0, w2, a0, a2, k_k, k_a, r_k, gn_w, gn_b):
    B, T, _ = r.shape
    f32 = jnp.float32
    r, k, v = r.astype(f32), k.astype(f32), v.astype(f32)
    w_raw = w0.astype(f32) + jnp.einsum('btdr,drc->btdc', jnp.tanh(wlo.astype(f32)), w2.astype(f32))
    log_w = -jax.nn.softplus(-w_raw) - 0.5
    decay = jnp.exp(-jnp.exp(log_w))
    a = jax.nn.sigmoid(a0.astype(f32) + jnp.einsum('btdr,drc->btdc', alo.astype(f32), a2.astype(f32)))
    kk = (k * k_k.astype(f32)).reshape(B, T, H_A, HEAD_A)
    kk = kk * lax.rsqrt(jnp.maximum(jnp.sum(kk * kk, -1, keepdims=True), 1e-24))
    k_dir = k[:, :, None, :] * (1.0 + (a - 1.0) * k_a.astype(f32))
    hs = (B, T, 2, H_A, HEAD_A)
    a = a.reshape(hs)
    k_dir = k_dir.reshape(hs)
    decay = decay.reshape(hs)
    rh = r.reshape(B, T, H_A, HEAD_A)
    vh = v.reshape(B, T, H_A, HEAD_A)
    bb = kk[:, :, None] * a
    xs = (_dirs_shared(rh), _dirs(decay), _dirs(k_dir), _dirs_shared(vh),
          _dirs_shared(-kk), _dirs(bb))
    S0 = jnp.zeros((2, B, H_A, HEAD_A, HEAD_A), f32)
    _, o = lax.scan(_rwkv7_step, S0, xs)
    o = o.transpose(1, 2, 0, 3, 4)
    o = o[0] + jnp.flip(o[1], 1)
    mu = jnp.mean(o, -1, keepdims=True)
    var = jnp.mean(jnp.square(o - mu), -1, keepdims=True)
    o = (o - mu) * lax.rsqrt(var + GN_EPS)
    o = o * gn_w.astype(f32).reshape(H_A, HEAD_A) + gn_b.astype(f32).reshape(H_A, HEAD_A)
    bonus = jnp.sum(rh[:, :, None] * k_dir * r_k.astype(f32), -1, keepdims=True) * vh[:, :, None]
    o = o + jnp.sum(bonus, axis=2)
    return o.reshape(B, T, D_A)


def neighbourhood_attention(q, k, v, rpb):
    B, T, _ = q.shape
    rows = T // GRID_W
    kr = min(MAX_KR, rows)
    g = (B, rows, GRID_W, H_B, HEAD_B)
    q = (q * (HEAD_B ** -0.5)).reshape(g)
    k = k.reshape(g)
    v = v.reshape(g)
    c0 = np.arange(N_CB) * Q_BLK_W
    kb = np.clip(c0 - WIN_C // 2, 0, GRID_W - K_BLK_W)
    key_cols = kb[:, None] + np.arange(K_BLK_W)
    q_cols = c0[:, None] + np.arange(Q_BLK_W)
    cs = np.clip(q_cols - WIN_C // 2, 0, GRID_W - WIN_C)
    kc = key_cols[:, None, :]
    col_mask = (kc >= cs[..., None]) & (kc < cs[..., None] + WIN_C)
    dc_idx = np.clip(kc - q_cols[..., None] + WIN_C - 1, 0, 2 * WIN_C - 2)
    rpb32 = rpb.astype(jnp.float32)

    def row_fn(r):
        rs = jnp.clip(r - kr // 2, 0, rows - kr)
        k_rows = lax.dynamic_slice_in_dim(k, rs, kr, axis=1)
        v_rows = lax.dynamic_slice_in_dim(v, rs, kr, axis=1)
        k_blk = k_rows[:, :, key_cols]
        v_blk = v_rows[:, :, key_cols]
        q_blk = lax.dynamic_index_in_dim(q, r, axis=1, keepdims=False).reshape(B, N_CB, Q_BLK_W, H_B, HEAD_B)
        s = jnp.einsum('bnqhd,binjhd->bhnqij', q_blk, k_blk, preferred_element_type=jnp.float32)
        dr_idx = rs + jnp.arange(kr) - r + (MAX_KR - 1)
        bias = rpb32[:, dr_idx][:, :, dc_idx].transpose(0, 2, 3, 1, 4)
        s = jnp.where(col_mask[None, None, :, :, None, :], s + bias[None], -1e30)
        p = jax.nn.softmax(s.reshape(B, H_B, N_CB, Q_BLK_W, kr * K_BLK_W), axis=-1)
        p = p.reshape(B, H_B, N_CB, Q_BLK_W, kr, K_BLK_W).astype(v.dtype)
        o = jnp.einsum('bhnqij,binjhd->bnqhd', p, v_blk)
        return o.reshape(B, GRID_W, H_B, HEAD_B)

    out = lax.map(row_fn, jnp.arange(rows))
    return jnp.moveaxis(out, 0, 1).reshape(B, T, D_B)


def hybrid_layer(x, norm_g, w_in, shift_mu, w0, w2, a0, a2, k_k, k_a, r_k, gn_w, gn_b,
                 rpb, w_pa, w_pb, w_out):
    B, T, _ = x.shape
    h = rms_norm(x, norm_g)
    z = jnp.einsum('btd,de->bte', h, w_in)
    za = centred_shift(z[..., :A_SHIFT], shift_mu)
    ra = za[..., 0:D_A]
    ka = za[..., D_A:2 * D_A]
    va = za[..., 2 * D_A:3 * D_A]
    wlo = za[..., 3 * D_A:3 * D_A + 2 * LORA_W].reshape(B, T, 2, LORA_W)
    alo = za[..., 3 * D_A + 2 * LORA_W:A_SHIFT].reshape(B, T, 2, LORA_A)
    ga = z[..., OFF_GA:OFF_QB]
    qb = z[..., OFF_QB:OFF_KB]
    kb = z[..., OFF_KB:OFF_VB]
    vb = z[..., OFF_VB:OFF_GB]
    gb = z[..., OFF_GB:OFF_MA]
    ma = z[..., OFF_MA:OFF_MB]
    mb = z[..., OFF_MB:D_IN]
    ya = rwkv7_bidir(ra, ka, va, wlo, alo, w0, w2, a0, a2, k_k, k_a, r_k, gn_w, gn_b)
    ya = (ya * jax.nn.silu(ga.astype(jnp.float32))).astype(x.dtype)
    yb = neighbourhood_attention(qb, kb, vb, rpb) * jax.nn.silu(gb)
    merged = (jax.nn.sigmoid(ma) * jnp.einsum('btc,cd->btd', ya, w_pa)
              + jax.nn.sigmoid(mb) * jnp.einsum('btc,cd->btd', yb, w_pb))
    return x + jnp.einsum('btd,de->bte', merged, w_out)


def trunk(x, norm_g, w_in, shift_mu, w0, w2, a0, a2, k_k, k_a, r_k, gn_w, gn_b,
          rpb, w_pa, w_pb, w_out, final_g):
    for l in range(DEPTH):
        x = hybrid_layer(x, norm_g[l], w_in[l], shift_mu[l], w0[l], w2[l], a0[l], a2[l],
                         k_k[l], k_a[l], r_k[l], gn_w[l], gn_b[l], rpb[l],
                         w_pa[l], w_pb[l], w_out[l])
    return rms_norm(x, final_g)


def setup_inputs(seed: int = 0) -> dict:
    key = jax.random.key(seed)
    ks = jax.random.split(key, 20)
    f32 = jnp.float32
    nrm = lambda k, s: jax.random.normal(k, s, f32)
    L = DEPTH
    return {
        'x_prompt': nrm(ks[0], (BATCH, SEQ, D_MODEL)),
        'x_sample': nrm(ks[1], (DEC_BATCH, DEC_SEQ, D_MODEL)),
        'norm_g': 1.0 + 0.02 * nrm(ks[2], (L, D_MODEL)),
        'w_in': nrm(ks[3], (L, D_MODEL, D_IN)) * D_MODEL ** -0.5,
        'shift_mu': jax.random.uniform(ks[4], (L, A_SHIFT), f32),
        'w0': jax.random.uniform(ks[5], (L, 2, D_A), f32, -6.0, 1.0),
        'w2': nrm(ks[6], (L, 2, LORA_W, D_A)) * (0.3 * LORA_W ** -0.5),
        'a0': 0.5 * nrm(ks[7], (L, 2, D_A)),
        'a2': nrm(ks[8], (L, 2, LORA_A, D_A)) * (0.3 * LORA_A ** -0.5),
        'k_k': 0.85 + 0.05 * nrm(ks[9], (L, D_A)),
        'k_a': 1.0 + 0.05 * nrm(ks[10], (L, D_A)),
        'r_k': 0.1 * nrm(ks[11], (L, H_A, HEAD_A)),
        'gn_w': 1.0 + 0.02 * nrm(ks[12], (L, D_A)),
        'gn_b': 0.02 * nrm(ks[13], (L, D_A)),
        'rpb': 0.1 * nrm(ks[14], (L, H_B, 2 * MAX_KR - 1, 2 * WIN_C - 1)),
        'w_pa': nrm(ks[15], (L, D_A, D_MODEL)) * D_A ** -0.5,
        'w_pb': nrm(ks[16], (L, D_B, D_MODEL)) * D_B ** -0.5,
        'w_out': nrm(ks[17], (L, D_MODEL, D_MODEL)) * D_MODEL ** -0.5,
        'final_g': 1.0 + 0.02 * nrm(ks[18], (D_MODEL,)),
    }


def reference(x_prompt, x_sample, norm_g, w_in, shift_mu, w0, w2, a0, a2, k_k, k_a, r_k,
              gn_w, gn_b, rpb, w_pa, w_pb, w_out, final_g):
    y_prompt = trunk(x_prompt, norm_g, w_in, shift_mu, w0, w2, a0, a2, k_k, k_a, r_k,
                     gn_w, gn_b, rpb, w_pa, w_pb, w_out, final_g)
    y_sample = trunk(x_sample, norm_g, w_in, shift_mu, w0, w2, a0, a2, k_k, k_a, r_k,
                     gn_w, gn_b, rpb, w_pa, w_pb, w_out, final_g)
    return (y_prompt, y_sample)
```

```python
import functools
import math

import numpy as np
import jax
import jax.numpy as jnp
from jax import lax
from jax.experimental import pallas as pl
from jax.experimental.pallas import tpu as pltpu

f32 = jnp.float32
bf16 = jnp.bfloat16

D_MODEL = 1024
GRID_W = 64
D_A = 512
HEAD = 64
LORA = 64
GN_EPS = 6.4e-4
D_B = 512
MAX_KR = 8
WIN_C = 16
RMS_EPS = 1e-6
A_SHIFT = 3 * D_A + 4 * LORA
OFF_GA = A_SHIFT
OFF_QB = OFF_GA + D_A
OFF_GB = OFF_QB + 3 * D_B
OFF_MA = OFF_GB + D_B
D_IN = OFF_MA + 2 * D_MODEL

LANES = 128
N_PAIR = D_A // LANES
CHUNK = 64
BF16_ROWS = 16
TOKEN_TILE = 512
VMEM_LIMIT = 56 * 1024 * 1024
LOG_DECAY_SCALE = -math.exp(-0.5)

_NT = (((1,), (1,)), ((), ()))


def _sigmoid(x):
    return 1.0 / (1.0 + jnp.exp(-x))


def _dot(a, b):
    return jnp.dot(a, b, preferred_element_type=f32)


def _dot_nt(a, b):
    return lax.dot_general(a, b, _NT, preferred_element_type=f32)


def _rms_norm(x, g):
    return x * lax.rsqrt(jnp.mean(x * x, axis=-1, keepdims=True) + RMS_EPS) * g


_IN_SEGMENTS = ((0, OFF_GA), (OFF_GA, OFF_QB), (OFF_QB, OFF_GB), (OFF_GB, OFF_MA), (OFF_MA, D_IN))


def _inproj_kernel(x_ref, g_ref, w_ref, *out_refs):
    h = _rms_norm(x_ref[...], g_ref[...]).astype(bf16)
    for ref, (lo, hi) in zip(out_refs, _IN_SEGMENTS):
        ref[...] = _dot(h, w_ref[:, lo:hi]).astype(bf16)


def _inproj(x2d, g, w_bf16):
    n_tok = x2d.shape[0]
    tm = TOKEN_TILE
    const = lambda i: (0, 0)
    row = lambda i: (i, 0)
    return pl.pallas_call(
        _inproj_kernel,
        grid=(n_tok // tm,),
        in_specs=[
            pl.BlockSpec((tm, D_MODEL), row),
            pl.BlockSpec((1, D_MODEL), const),
            pl.BlockSpec((D_MODEL, D_IN), const, pipeline_mode=pl.Buffered(1)),
        ],
        out_specs=[pl.BlockSpec((tm, hi - lo), row) for lo, hi in _IN_SEGMENTS],
        out_shape=[jax.ShapeDtypeStruct((n_tok, hi - lo), bf16) for lo, hi in _IN_SEGMENTS],
        compiler_params=pltpu.CompilerParams(
            dimension_semantics=("arbitrary",), vmem_limit_bytes=VMEM_LIMIT),
        name="inproj",
    )(x2d, g, w_bf16)


def _rwkv_kernel(za_ref, ga_ref, mu_ref, w0_ref, w2_ref, a0_ref, a2_ref, kk_ref, ka_ref, rk_ref,
                 gnw_ref, gnb_ref, out_ref, g_scr, o_scr, bonus_scr, *, seq_len):
    C = CHUNK
    n_chunks = seq_len // C
    lane = lax.broadcasted_iota(jnp.int32, (1, LANES), 1)
    head0 = lane < HEAD
    row_c = lax.broadcasted_iota(jnp.int32, (C, 1), 0)
    t_i = lax.broadcasted_iota(jnp.int32, (C, 2 * C), 0)
    s_i = lax.broadcasted_iota(jnp.int32, (C, 2 * C), 1) % C
    strict = (t_i > s_i, t_i < s_i)
    incl = (t_i >= s_i, t_i <= s_i)
    eye_pair = (t_i == s_i).astype(f32)
    r2 = lax.broadcasted_iota(jnp.int32, (2 * C, 2 * C), 0)
    c2 = lax.broadcasted_iota(jnp.int32, (2 * C, 2 * C), 1)
    blockdiag = (r2 // C) == (c2 // C)
    seg_ones = blockdiag.astype(bf16)
    tt = lax.broadcasted_iota(jnp.int32, (C, C), 0)
    ss = lax.broadcasted_iota(jnp.int32, (C, C), 1)
    tri = ((tt >= ss).astype(bf16), (tt <= ss).astype(bf16))

    def stack(x):
        return jnp.concatenate([jnp.where(head0, x, 0.0), jnp.where(head0, 0.0, x)], axis=0).astype(bf16)

    def blockdiag_of(x):
        return jnp.where(blockdiag, jnp.concatenate([x, x], axis=0), 0.0).astype(bf16)

    def seg_sum(x):
        return jnp.concatenate(
            [_dot(x[:, p * LANES:(p + 1) * LANES].astype(bf16), seg_ones) for p in range(N_PAIR)], axis=1)

    def shifted_chunk(t0):
        zc = za_ref[pl.ds(t0, C), :].astype(f32)
        tp = pl.multiple_of(jnp.maximum(t0 - BF16_ROWS, 0), BF16_ROWS)
        prev_row = za_ref[pl.ds(tp, BF16_ROWS), :][BF16_ROWS - 1:, :].astype(f32)
        prev_row = jnp.where(t0 > 0, prev_row, 0.0)
        tn = pl.multiple_of(jnp.minimum(t0 + C, seq_len - BF16_ROWS), BF16_ROWS)
        next_row = za_ref[pl.ds(tn, BF16_ROWS), :][:1, :].astype(f32)
        next_row = jnp.where(t0 + C < seq_len, next_row, 0.0)
        zp = jnp.where(row_c == 0, prev_row, pltpu.roll(zc, 1, 0))
        zn = jnp.where(row_c == C - 1, next_row, pltpu.roll(zc, C - 1, 0))
        return zc + mu_ref[...] * (0.5 * (zp + zn) - zc)

    g_scr[...] = jnp.zeros_like(g_scr)

    def direction_step(d, t0):
        z = shifted_chunk(t0)
        r = z[:, 0:D_A]
        k = z[:, D_A:2 * D_A]
        v = z[:, 2 * D_A:3 * D_A]
        w_lora = z[:, 3 * D_A:3 * D_A + 2 * LORA]
        a_lora = z[:, 3 * D_A + 2 * LORA:A_SHIFT]
        w_raw = w0_ref[d:d + 1, :] + _dot(jnp.tanh(w_lora).astype(bf16), w2_ref[d])
        ld = LOG_DECAY_SCALE * _sigmoid(w_raw)
        a = _sigmoid(a0_ref[d:d + 1, :] + _dot(a_lora.astype(bf16), a2_ref[d]))
        kk = k * kk_ref[...]
        kk = kk * lax.rsqrt(jnp.maximum(seg_sum(kk * kk), 1e-24))
        k_dir = k * (1.0 + (a - 1.0) * ka_ref[...])
        bonus = seg_sum(r * k_dir * rk_ref[...]) * v
        ld_hi = ld.astype(bf16)
        ld_lo = (ld - ld_hi.astype(f32)).astype(bf16)
        cs = _dot(tri[d], ld_hi) + _dot(tri[d], ld_lo)
        e_pos = jnp.exp(cs)
        e_neg = jnp.exp(-cs)
        e_prev = jnp.exp(cs - ld)
        last = C - 1 if d == 0 else 0
        p_tot = e_pos[last:last + 1, :]
        r_t = r * e_pos
        a_t = -kk * e_prev
        b_t = kk * a * e_neg
        k_t = k_dir * e_neg
        b_hat = b_t * p_tot
        k_hat = k_t * p_tot
        outs = []
        for p in range(N_PAIR):
            sl = slice(p * LANES, (p + 1) * LANES)
            x_all = jnp.concatenate([a_t[:, sl], r_t[:, sl]], axis=0).astype(bf16)
            y_all = jnp.concatenate([stack(b_t[:, sl]), stack(k_t[:, sl])], axis=0)
            s = _dot_nt(x_all, y_all)
            a_ab = jnp.where(strict[d], s[:C, :2 * C], 0.0)
            a_ak = jnp.where(strict[d], s[:C, 2 * C:], 0.0)
            a_rb = jnp.where(incl[d], s[C:, :2 * C], 0.0)
            a_rk = jnp.where(incl[d], s[C:, 2 * C:], 0.0)
            t_inv = eye_pair + a_ab
            a_pow = a_ab
            n = 1
            while 2 * n < C:
                a_pow = _dot(a_pow.astype(bf16), blockdiag_of(a_pow))
                t_inv = t_inv + _dot(t_inv.astype(bf16), blockdiag_of(a_pow))
                n *= 2
            v_stack = stack(v[:, sl])
            x0 = _dot(a_ak.astype(bf16), v_stack)
            wu = _dot(t_inv.astype(bf16), jnp.concatenate([stack(a_t[:, sl]), stack(x0)], axis=1))
            w_mat = wu[:, :LANES]
            u0 = wu[:, LANES:]
            g = g_scr[d, p]
            g_b = g.astype(bf16)
            u = _dot_nt(w_mat.astype(bf16), g_b) + u0
            o = (_dot_nt(r_t[:, sl].astype(bf16), g_b) + _dot(a_rb.astype(bf16), stack(u))
                 + _dot(a_rk.astype(bf16), v_stack))
            uv_t = jnp.concatenate([u, v[:, sl]], axis=0).T.astype(bf16)
            bk = jnp.concatenate([b_hat[:, sl], k_hat[:, sl]], axis=0).astype(bf16)
            g_scr[d, p] = g * p_tot[:, sl] + jnp.where(blockdiag, _dot(uv_t, bk), 0.0)
            outs.append(o)
        o_scr[d, pl.ds(t0, C), :] = jnp.concatenate(outs, axis=1)
        bonus_scr[d, pl.ds(t0, C), :] = bonus

    def step(i, carry):
        direction_step(0, pl.multiple_of(i * C, C))
        direction_step(1, pl.multiple_of((n_chunks - 1 - i) * C, C))
        return carry

    lax.fori_loop(0, n_chunks, step, 0)

    def finish(i, carry):
        t0 = pl.multiple_of(i * C, C)
        rows = pl.ds(t0, C)
        o = o_scr[0, rows, :] + o_scr[1, rows, :]
        mean = seg_sum(o) * (1.0 / HEAD)
        oc = o - mean
        var = seg_sum(oc * oc) * (1.0 / HEAD)
        y = oc * lax.rsqrt(var + GN_EPS) * gnw_ref[...] + gnb_ref[...]
        y = y + bonus_scr[0, rows, :] + bonus_scr[1, rows, :]
        gate = ga_ref[rows, :].astype(f32)
        out_ref[rows, :] = (y * (gate * _sigmoid(gate))).astype(bf16)
        return carry

    lax.fori_loop(0, n_chunks, finish, 0)


def _rwkv(za, ga, mu, w0, w2p, a0, a2p, k_k, k_a, r_k, gn_w, gn_b):
    B, T, _ = za.shape
    seq = lambda width: pl.BlockSpec((None, T, width), lambda b: (b, 0, 0))
    full = lambda *shape: pl.BlockSpec(shape, lambda b: (0,) * len(shape))
    return pl.pallas_call(
        functools.partial(_rwkv_kernel, seq_len=T),
        grid=(B,),
        in_specs=[
            seq(A_SHIFT), seq(D_A), full(1, A_SHIFT), full(2, D_A), full(2, 2 * LORA, D_A),
            full(2, D_A), full(2, 2 * LORA, D_A), full(1, D_A), full(1, D_A), full(1, D_A),
            full(1, D_A), full(1, D_A),
        ],
        out_specs=seq(D_A),
        out_shape=jax.ShapeDtypeStruct((B, T, D_A), bf16),
        scratch_shapes=[
            pltpu.VMEM((2, N_PAIR, LANES, LANES), f32),
            pltpu.VMEM((2, T, D_A), f32),
            pltpu.VMEM((2, T, D_A), f32),
        ],
        compiler_params=pltpu.CompilerParams(
            dimension_semantics=("arbitrary",), vmem_limit_bytes=VMEM_LIMIT),
        name="rwkv7",
    )(za, ga, mu, w0, w2p, a0, a2p, k_k, k_a, r_k, gn_w, gn_b)


def _natten_bias_table(rpb):
    qc = np.arange(GRID_W)
    kc = np.arange(GRID_W)
    cs = np.clip(qc - WIN_C // 2, 0, GRID_W - WIN_C)
    col_mask = (kc[None, :] >= cs[:, None]) & (kc[None, :] < cs[:, None] + WIN_C)
    dc = np.clip(kc[None, :] - qc[:, None] + WIN_C - 1, 0, 2 * WIN_C - 2)
    tbl = jnp.where(col_mask[None, None], rpb.astype(f32)[:, :, dc], -1e30)
    variants = [tbl[:, d0:d0 + MAX_KR].transpose(0, 2, 1, 3).reshape(rpb.shape[0], GRID_W, MAX_KR * GRID_W)
                for d0 in range(MAX_KR)]
    return jnp.stack(variants, axis=0)


def _natten_kernel(qkv_ref, gb_ref, tbl_ref, out_ref, *, seq_len):
    n_rows = seq_len // GRID_W
    n_keys = MAX_KR * GRID_W
    lane = lax.broadcasted_iota(jnp.int32, (1, LANES), 1)
    head0 = lane < HEAD

    def row_step(r, carry):
        rs = jnp.clip(r - MAX_KR // 2, 0, n_rows - MAX_KR)
        d0 = rs - r + (MAX_KR - 1)
        q_rows = pl.ds(pl.multiple_of(r * GRID_W, GRID_W), GRID_W)
        k_rows = pl.ds(pl.multiple_of(rs * GRID_W, GRID_W), n_keys)
        for p in range(N_PAIR):
            q = qkv_ref[q_rows, p * LANES:(p + 1) * LANES] * (HEAD ** -0.5)
            k = qkv_ref[k_rows, D_B + p * LANES:D_B + (p + 1) * LANES]
            v = qkv_ref[k_rows, 2 * D_B + p * LANES:2 * D_B + (p + 1) * LANES]
            halves = []
            for hh in range(2):
                keep = head0 if hh == 0 else jnp.logical_not(head0)
                qm = jnp.where(keep, q, jnp.zeros_like(q))
                s = _dot_nt(qm, k) + tbl_ref[d0, 2 * p + hh]
                e = jnp.exp(s - jnp.max(s, axis=-1, keepdims=True))
                den = jnp.sum(e, axis=-1, keepdims=True)
                halves.append(_dot(e.astype(bf16), v) * (1.0 / den))
            o = jnp.where(head0, halves[0], halves[1])
            gate = gb_ref[q_rows, p * LANES:(p + 1) * LANES].astype(f32)
            out_ref[q_rows, p * LANES:(p + 1) * LANES] = (o * (gate * _sigmoid(gate))).astype(bf16)
        return carry

    lax.fori_loop(0, n_rows, row_step, 0)


def _natten(qkv, gb, tbl):
    B, T, _ = qkv.shape
    seq = lambda width: pl.BlockSpec((None, T, width), lambda b: (b, 0, 0))
    return pl.pallas_call(
        functools.partial(_natten_kernel, seq_len=T),
        grid=(B,),
        in_specs=[seq(3 * D_B), seq(D_B),
                  pl.BlockSpec(tbl.shape, lambda b: (0, 0, 0, 0), pipeline_mode=pl.Buffered(1))],
        out_specs=seq(D_B),
        out_shape=jax.ShapeDtypeStruct((B, T, D_B), bf16),
        compiler_params=pltpu.CompilerParams(
            dimension_semantics=("arbitrary",), vmem_limit_bytes=VMEM_LIMIT),
        name="natten",
    )(qkv, gb, tbl)


def _merge_kernel(x_ref, ya_ref, yb_ref, m_ref, wpa_ref, wpb_ref, wout_ref, fg_ref, out_ref, *, final_norm):
    m = m_ref[...].astype(f32)
    merged = (_sigmoid(m[:, :D_MODEL]) * _dot(ya_ref[...], wpa_ref[...])
              + _sigmoid(m[:, D_MODEL:]) * _dot(yb_ref[...], wpb_ref[...]))
    y = x_ref[...] + _dot(merged.astype(bf16), wout_ref[...])
    if final_norm:
        y = _rms_norm(y, fg_ref[...])
    out_ref[...] = y


def _merge(x2d, ya, yb, m, w_pa, w_pb, w_out, final_g, final_norm):
    n_tok = x2d.shape[0]
    tm = TOKEN_TILE
    const = lambda i: (0, 0)
    row = lambda i: (i, 0)
    return pl.pallas_call(
        functools.partial(_merge_kernel, final_norm=final_norm),
        grid=(n_tok // tm,),
        in_specs=[
            pl.BlockSpec((tm, D_MODEL), row), pl.BlockSpec((tm, D_A), row), pl.BlockSpec((tm, D_B), row),
            pl.BlockSpec((tm, 2 * D_MODEL), row),
            pl.BlockSpec((D_A, D_MODEL), const), pl.BlockSpec((D_B, D_MODEL), const),
            pl.BlockSpec((D_MODEL, D_MODEL), const), pl.BlockSpec((1, D_MODEL), const),
        ],
        out_specs=pl.BlockSpec((tm, D_MODEL), row),
        out_shape=jax.ShapeDtypeStruct((n_tok, D_MODEL), f32),
        compiler_params=pltpu.CompilerParams(
            dimension_semantics=("arbitrary",), vmem_limit_bytes=VMEM_LIMIT),
        name="merge",
    )(x2d, ya, yb, m, w_pa, w_pb, w_out, final_g)


def _pad_lora(w):
    z = jnp.zeros_like(w[0])
    return jnp.stack([jnp.concatenate([w[0], z], axis=0), jnp.concatenate([z, w[1]], axis=0)], axis=0).astype(bf16)


def _trunk(x, norm_g, w_in, shift_mu, w0, w2, a0, a2, k_k, k_a, r_k, gn_w, gn_b, rpb, w_pa, w_pb, w_out,
           final_g):
    B, T, _ = x.shape
    depth = norm_g.shape[0]
    x2d = x.reshape(B * T, D_MODEL)
    row = lambda a: a.reshape(1, -1).astype(f32)
    for l in range(depth):
        za, ga, qkv, gb, m = _inproj(x2d, row(norm_g[l]), w_in[l].astype(bf16))
        ya = _rwkv(za.reshape(B, T, A_SHIFT), ga.reshape(B, T, D_A), row(shift_mu[l]), w0[l].astype(f32),
                   _pad_lora(w2[l]), a0[l].astype(f32), _pad_lora(a2[l]), row(k_k[l]), row(k_a[l]),
                   row(r_k[l]), row(gn_w[l]), row(gn_b[l]))
        yb = _natten(qkv.reshape(B, T, 3 * D_B), gb.reshape(B, T, D_B), _natten_bias_table(rpb[l]))
        x2d = _merge(x2d, ya.reshape(B * T, D_A), yb.reshape(B * T, D_B), m, w_pa[l].astype(bf16),
                     w_pb[l].astype(bf16), w_out[l].astype(bf16), row(final_g), l == depth - 1)
    return x2d.reshape(B, T, D_MODEL)


def kernel(x_prompt, x_sample, norm_g, w_in, shift_mu, w0, w2, a0, a2, k_k, k_a, r_k, gn_w, gn_b, rpb, w_pa,
           w_pb, w_out, final_g):
    params = (norm_g, w_in, shift_mu, w0, w2, a0, a2, k_k, k_a, r_k, gn_w, gn_b, rpb, w_pa, w_pb, w_out,
              final_g)
    return (_trunk(x_prompt, *params), _trunk(x_sample, *params))
```

```python
import functools
import math

import numpy as np
import jax
import jax.numpy as jnp
from jax import lax
from jax.experimental import pallas as pl
from jax.experimental.pallas import tpu as pltpu

f32 = jnp.float32
bf16 = jnp.bfloat16

D_MODEL = 1024
GRID_W = 64
D_A = 512
HEAD = 64
LORA = 64
GN_EPS = 6.4e-4
D_B = 512
MAX_KR = 8
WIN_C = 16
RMS_EPS = 1e-6
A_SHIFT = 3 * D_A + 4 * LORA
OFF_GA = A_SHIFT
OFF_QB = OFF_GA + D_A
OFF_GB = OFF_QB + 3 * D_B
OFF_MA = OFF_GB + D_B
D_IN = OFF_MA + 2 * D_MODEL

LANES = 128
N_PAIR = D_A // LANES
CHUNK = 64
BF16_ROWS = 16
TOKEN_TILE = 512
VMEM_LIMIT = 56 * 1024 * 1024
LOG_DECAY_SCALE = -math.exp(-0.5)

_NT = (((1,), (1,)), ((), ()))


def _sigmoid(x):
    return 1.0 / (1.0 + jnp.exp(-x))


def _dot(a, b):
    return jnp.dot(a, b, preferred_element_type=f32)


def _dot_nt(a, b):
    return lax.dot_general(a, b, _NT, preferred_element_type=f32)


def _rms_norm(x, g):
    return x * lax.rsqrt(jnp.mean(x * x, axis=-1, keepdims=True) + RMS_EPS) * g


_IN_SEGMENTS = ((0, OFF_GA), (OFF_GA, OFF_QB), (OFF_QB, OFF_GB), (OFF_GB, OFF_MA), (OFF_MA, D_IN))


def _inproj_kernel(x_ref, g_ref, w_ref, *out_refs):
    h = _rms_norm(x_ref[...], g_ref[...]).astype(bf16)
    for ref, (lo, hi) in zip(out_refs, _IN_SEGMENTS):
        ref[...] = _dot(h, w_ref[:, lo:hi]).astype(bf16)


def _inproj(x2d, g, w_bf16):
    n_tok = x2d.shape[0]
    tm = TOKEN_TILE
    const = lambda i: (0, 0)
    row = lambda i: (i, 0)
    return pl.pallas_call(
        _inproj_kernel,
        grid=(n_tok // tm,),
        in_specs=[
            pl.BlockSpec((tm, D_MODEL), row),
            pl.BlockSpec((1, D_MODEL), const),
            pl.BlockSpec((D_MODEL, D_IN), const, pipeline_mode=pl.Buffered(1)),
        ],
        out_specs=[pl.BlockSpec((tm, hi - lo), row) for lo, hi in _IN_SEGMENTS],
        out_shape=[jax.ShapeDtypeStruct((n_tok, hi - lo), bf16) for lo, hi in _IN_SEGMENTS],
        compiler_params=pltpu.CompilerParams(
            dimension_semantics=("arbitrary",), vmem_limit_bytes=VMEM_LIMIT),
        name="inproj",
    )(x2d, g, w_bf16)


def _rwkv_kernel(za_ref, ga_ref, mu_ref, w0_ref, w2_ref, a0_ref, a2_ref, kk_ref, ka_ref, rk_ref,
                 gnw_ref, gnb_ref, out_ref, g_scr, o_scr, bonus_scr, *, seq_len):
    C = CHUNK
    n_chunks = seq_len // C
    lane = lax.broadcasted_iota(jnp.int32, (1, LANES), 1)
    head0 = lane < HEAD
    row_c = lax.broadcasted_iota(jnp.int32, (C, 1), 0)
    t_i = lax.broadcasted_iota(jnp.int32, (C, 2 * C), 0)
    s_i = lax.broadcasted_iota(jnp.int32, (C, 2 * C), 1) % C
    strict = (t_i > s_i, t_i < s_i)
    incl = (t_i >= s_i, t_i <= s_i)
    eye_pair = (t_i == s_i).astype(f32)
    r2 = lax.broadcasted_iota(jnp.int32, (2 * C, 2 * C), 0)
    c2 = lax.broadcasted_iota(jnp.int32, (2 * C, 2 * C), 1)
    blockdiag = (r2 // C) == (c2 // C)
    seg_ones = blockdiag.astype(bf16)
    tt = lax.broadcasted_iota(jnp.int32, (C, C), 0)
    ss = lax.broadcasted_iota(jnp.int32, (C, C), 1)
    tri = ((tt >= ss).astype(bf16), (tt <= ss).astype(bf16))

    def stack(x):
        return jnp.concatenate([jnp.where(head0, x, 0.0), jnp.where(head0, 0.0, x)], axis=0).astype(bf16)

    def blockdiag_of(x):
        return jnp.where(blockdiag, jnp.concatenate([x, x], axis=0), 0.0).astype(bf16)

    def seg_sum(x):
        return jnp.concatenate(
            [_dot(x[:, p * LANES:(p + 1) * LANES].astype(bf16), seg_ones) for p in range(N_PAIR)], axis=1)

    def shifted_chunk(t0):
        zc = za_ref[pl.ds(t0, C), :].astype(f32)
        tp = pl.multiple_of(jnp.maximum(t0 - BF16_ROWS, 0), BF16_ROWS)
        prev_row = za_ref[pl.ds(tp, BF16_ROWS), :][BF16_ROWS - 1:, :].astype(f32)
        prev_row = jnp.where(t0 > 0, prev_row, 0.0)
        tn = pl.multiple_of(jnp.minimum(t0 + C, seq_len - BF16_ROWS), BF16_ROWS)
        next_row = za_ref[pl.ds(tn, BF16_ROWS), :][:1, :].astype(f32)
        next_row = jnp.where(t0 + C < seq_len, next_row, 0.0)
        zp = jnp.where(row_c == 0, prev_row, pltpu.roll(zc, 1, 0))
        zn = jnp.where(row_c == C - 1, next_row, pltpu.roll(zc, C - 1, 0))
        return zc + mu_ref[...] * (0.5 * (zp + zn) - zc)

    g_scr[...] = jnp.zeros_like(g_scr)

    dirs = (0, 1)
    chains = [(d, p) for d in dirs for p in range(N_PAIR)]
    pair_lanes = [slice(p * LANES, (p + 1) * LANES) for p in range(N_PAIR)]

    def step(i, carry):
        t0 = (pl.multiple_of(i * C, C), pl.multiple_of((n_chunks - 1 - i) * C, C))
        z = [shifted_chunk(t0[d]) for d in dirs]
        r = [z[d][:, 0:D_A] for d in dirs]
        k = [z[d][:, D_A:2 * D_A] for d in dirs]
        v = [z[d][:, 2 * D_A:3 * D_A] for d in dirs]
        w_lo = [_dot(jnp.tanh(z[d][:, 3 * D_A:3 * D_A + 2 * LORA]).astype(bf16), w2_ref[d]) for d in dirs]
        a_lo = [_dot(z[d][:, 3 * D_A + 2 * LORA:A_SHIFT].astype(bf16), a2_ref[d]) for d in dirs]
        kk = [k[d] * kk_ref[...] for d in dirs]
        kk_ss = [seg_sum(kk[d] * kk[d]) for d in dirs]
        ld = [LOG_DECAY_SCALE * _sigmoid(w0_ref[d:d + 1, :] + w_lo[d]) for d in dirs]
        ld_hi = [ld[d].astype(bf16) for d in dirs]
        ld_lo = [(ld[d] - ld_hi[d].astype(f32)).astype(bf16) for d in dirs]
        cs = [_dot(tri[d], ld_hi[d]) + _dot(tri[d], ld_lo[d]) for d in dirs]
        a = [_sigmoid(a0_ref[d:d + 1, :] + a_lo[d]) for d in dirs]
        kk = [kk[d] * lax.rsqrt(jnp.maximum(kk_ss[d], 1e-24)) for d in dirs]
        k_dir = [k[d] * (1.0 + (a[d] - 1.0) * ka_ref[...]) for d in dirs]
        bonus_s = [seg_sum(r[d] * k_dir[d] * rk_ref[...]) for d in dirs]
        e_pos = [jnp.exp(cs[d]) for d in dirs]
        e_neg = [jnp.exp(-cs[d]) for d in dirs]
        e_prev = [jnp.exp(cs[d] - ld[d]) for d in dirs]
        p_tot = [e_pos[0][C - 1:, :], e_pos[1][:1, :]]
        r_t = [r[d] * e_pos[d] for d in dirs]
        a_t = [-kk[d] * e_prev[d] for d in dirs]
        b_t = [kk[d] * a[d] * e_neg[d] for d in dirs]
        k_t = [k_dir[d] * e_neg[d] for d in dirs]
        b_hat = [b_t[d] * p_tot[d] for d in dirs]
        k_hat = [k_t[d] * p_tot[d] for d in dirs]
        for d in dirs:
            bonus_scr[d, pl.ds(t0[d], C), :] = bonus_s[d] * v[d]

        def per_chain(fn):
            return [fn(n, d, pair_lanes[p]) for n, (d, p) in enumerate(chains)]

        s = per_chain(lambda n, d, sl: _dot_nt(
            jnp.concatenate([a_t[d][:, sl], r_t[d][:, sl]], axis=0).astype(bf16),
            jnp.concatenate([stack(b_t[d][:, sl]), stack(k_t[d][:, sl])], axis=0)))
        a_ab = per_chain(lambda n, d, sl: jnp.where(strict[d], s[n][:C, :2 * C], 0.0))
        a_ak = per_chain(lambda n, d, sl: jnp.where(strict[d], s[n][:C, 2 * C:], 0.0).astype(bf16))
        a_rb = per_chain(lambda n, d, sl: jnp.where(incl[d], s[n][C:, :2 * C], 0.0).astype(bf16))
        a_rk = per_chain(lambda n, d, sl: jnp.where(incl[d], s[n][C:, 2 * C:], 0.0).astype(bf16))
        v_stack = per_chain(lambda n, d, sl: stack(v[d][:, sl]))
        x0 = per_chain(lambda n, d, sl: _dot(a_ak[n], v_stack[n]))
        a_pow = per_chain(lambda n, d, sl: _dot(a_ab[n].astype(bf16), blockdiag_of(a_ab[n])))
        t_inv = per_chain(lambda n, d, sl: eye_pair + a_ab[n])
        n_sq = 2
        while n_sq < C:
            last = 2 * n_sq >= C
            lhs = per_chain(lambda n, d, sl: (t_inv[n] if last else jnp.concatenate(
                [t_inv[n], a_pow[n]], axis=0)).astype(bf16))
            prod = per_chain(lambda n, d, sl: _dot(lhs[n], blockdiag_of(a_pow[n])))
            t_inv = per_chain(lambda n, d, sl: t_inv[n] + prod[n][:C])
            if not last:
                a_pow = per_chain(lambda n, d, sl: prod[n][C:])
            n_sq *= 2
        wu = per_chain(lambda n, d, sl: _dot(
            t_inv[n].astype(bf16), jnp.concatenate([stack(a_t[d][:, sl]), stack(x0[n])], axis=1)))
        g = [g_scr[d, p] for d, p in chains]
        g_b = [x.astype(bf16) for x in g]
        u = per_chain(lambda n, d, sl: _dot_nt(wu[n][:, :LANES].astype(bf16), g_b[n]) + wu[n][:, LANES:])
        o = per_chain(lambda n, d, sl: _dot_nt(r_t[d][:, sl].astype(bf16), g_b[n])
                      + _dot(a_rb[n], stack(u[n])) + _dot(a_rk[n], v_stack[n]))
        upd = per_chain(lambda n, d, sl: _dot(
            jnp.concatenate([u[n], v[d][:, sl]], axis=0).T.astype(bf16),
            jnp.concatenate([b_hat[d][:, sl], k_hat[d][:, sl]], axis=0).astype(bf16)))
        for n, (d, p) in enumerate(chains):
            g_scr[d, p] = g[n] * p_tot[d][:, pair_lanes[p]] + jnp.where(blockdiag, upd[n], 0.0)
        for d in dirs:
            o_scr[d, pl.ds(t0[d], C), :] = jnp.concatenate(o[d * N_PAIR:(d + 1) * N_PAIR], axis=1)
        return carry

    lax.fori_loop(0, n_chunks, step, 0)

    def finish(i, carry):
        t0 = pl.multiple_of(i * C, C)
        rows = pl.ds(t0, C)
        o = o_scr[0, rows, :] + o_scr[1, rows, :]
        mean = seg_sum(o) * (1.0 / HEAD)
        oc = o - mean
        var = seg_sum(oc * oc) * (1.0 / HEAD)
        y = oc * lax.rsqrt(var + GN_EPS) * gnw_ref[...] + gnb_ref[...]
        y = y + bonus_scr[0, rows, :] + bonus_scr[1, rows, :]
        gate = ga_ref[rows, :].astype(f32)
        out_ref[rows, :] = (y * (gate * _sigmoid(gate))).astype(bf16)
        return carry

    lax.fori_loop(0, n_chunks, finish, 0)


def _rwkv(za, ga, mu, w0, w2p, a0, a2p, k_k, k_a, r_k, gn_w, gn_b):
    B, T, _ = za.shape
    seq = lambda width: pl.BlockSpec((None, T, width), lambda b: (b, 0, 0))
    full = lambda *shape: pl.BlockSpec(shape, lambda b: (0,) * len(shape))
    return pl.pallas_call(
        functools.partial(_rwkv_kernel, seq_len=T),
        grid=(B,),
        in_specs=[
            seq(A_SHIFT), seq(D_A), full(1, A_SHIFT), full(2, D_A), full(2, 2 * LORA, D_A),
            full(2, D_A), full(2, 2 * LORA, D_A), full(1, D_A), full(1, D_A), full(1, D_A),
            full(1, D_A), full(1, D_A),
        ],
        out_specs=seq(D_A),
        out_shape=jax.ShapeDtypeStruct((B, T, D_A), bf16),
        scratch_shapes=[
            pltpu.VMEM((2, N_PAIR, LANES, LANES), f32),
            pltpu.VMEM((2, T, D_A), f32),
            pltpu.VMEM((2, T, D_A), f32),
        ],
        compiler_params=pltpu.CompilerParams(
            dimension_semantics=("arbitrary",), vmem_limit_bytes=VMEM_LIMIT),
        name="rwkv7",
    )(za, ga, mu, w0, w2p, a0, a2p, k_k, k_a, r_k, gn_w, gn_b)


def _natten_bias_table(rpb):
    qc = np.arange(GRID_W)
    kc = np.arange(GRID_W)
    cs = np.clip(qc - WIN_C // 2, 0, GRID_W - WIN_C)
    col_mask = (kc[None, :] >= cs[:, None]) & (kc[None, :] < cs[:, None] + WIN_C)
    dc = np.clip(kc[None, :] - qc[:, None] + WIN_C - 1, 0, 2 * WIN_C - 2)
    tbl = jnp.where(col_mask[None, None], rpb.astype(f32)[:, :, dc], -1e30)
    variants = [tbl[:, d0:d0 + MAX_KR].transpose(0, 2, 1, 3).reshape(rpb.shape[0], GRID_W, MAX_KR * GRID_W)
                for d0 in range(MAX_KR)]
    return jnp.stack(variants, axis=0)


def _natten_kernel(qkv_ref, gb_ref, tbl_ref, out_ref, *, seq_len):
    n_rows = seq_len // GRID_W
    n_keys = MAX_KR * GRID_W
    lane = lax.broadcasted_iota(jnp.int32, (1, LANES), 1)
    head0 = lane < HEAD

    def row_step(r, carry):
        rs = jnp.clip(r - MAX_KR // 2, 0, n_rows - MAX_KR)
        d0 = rs - r + (MAX_KR - 1)
        q_rows = pl.ds(pl.multiple_of(r * GRID_W, GRID_W), GRID_W)
        k_rows = pl.ds(pl.multiple_of(rs * GRID_W, GRID_W), n_keys)
        heads = [(p, hh) for p in range(N_PAIR) for hh in range(2)]
        lanes = [slice(p * LANES, (p + 1) * LANES) for p in range(N_PAIR)]
        q = [qkv_ref[q_rows, lanes[p]] * (HEAD ** -0.5) for p in range(N_PAIR)]
        s = [_dot_nt(jnp.where(head0 if hh == 0 else jnp.logical_not(head0), q[p], jnp.zeros_like(q[p])),
                     qkv_ref[k_rows, D_B + p * LANES:D_B + (p + 1) * LANES]) + tbl_ref[d0, 2 * p + hh]
             for p, hh in heads]
        e = [jnp.exp(x - jnp.max(x, axis=-1, keepdims=True)) for x in s]
        inv_den = [1.0 / jnp.sum(x, axis=-1, keepdims=True) for x in e]
        o = [_dot(e[n].astype(bf16), qkv_ref[k_rows, 2 * D_B + p * LANES:2 * D_B + (p + 1) * LANES]) * inv_den[n]
             for n, (p, hh) in enumerate(heads)]
        for p in range(N_PAIR):
            gate = gb_ref[q_rows, lanes[p]].astype(f32)
            pair_out = jnp.where(head0, o[2 * p], o[2 * p + 1])
            out_ref[q_rows, lanes[p]] = (pair_out * (gate * _sigmoid(gate))).astype(bf16)
        return carry

    lax.fori_loop(0, n_rows, row_step, 0)


def _natten(qkv, gb, tbl):
    B, T, _ = qkv.shape
    seq = lambda width: pl.BlockSpec((None, T, width), lambda b: (b, 0, 0))
    return pl.pallas_call(
        functools.partial(_natten_kernel, seq_len=T),
        grid=(B,),
        in_specs=[seq(3 * D_B), seq(D_B),
                  pl.BlockSpec(tbl.shape, lambda b: (0, 0, 0, 0), pipeline_mode=pl.Buffered(1))],
        out_specs=seq(D_B),
        out_shape=jax.ShapeDtypeStruct((B, T, D_B), bf16),
        compiler_params=pltpu.CompilerParams(
            dimension_semantics=("arbitrary",), vmem_limit_bytes=VMEM_LIMIT),
        name="natten",
    )(qkv, gb, tbl)


def _merge_kernel(x_ref, ya_ref, yb_ref, m_ref, wpa_ref, wpb_ref, wout_ref, fg_ref, out_ref, *, final_norm):
    m = m_ref[...].astype(f32)
    merged = (_sigmoid(m[:, :D_MODEL]) * _dot(ya_ref[...], wpa_ref[...])
              + _sigmoid(m[:, D_MODEL:]) * _dot(yb_ref[...], wpb_ref[...]))
    y = x_ref[...] + _dot(merged.astype(bf16), wout_ref[...])
    if final_norm:
        y = _rms_norm(y, fg_ref[...])
    out_ref[...] = y


def _merge(x2d, ya, yb, m, w_pa, w_pb, w_out, final_g, final_norm):
    n_tok = x2d.shape[0]
    tm = TOKEN_TILE
    const = lambda i: (0, 0)
    row = lambda i: (i, 0)
    return pl.pallas_call(
        functools.partial(_merge_kernel, final_norm=final_norm),
        grid=(n_tok // tm,),
        in_specs=[
            pl.BlockSpec((tm, D_MODEL), row), pl.BlockSpec((tm, D_A), row), pl.BlockSpec((tm, D_B), row),
            pl.BlockSpec((tm, 2 * D_MODEL), row),
            pl.BlockSpec((D_A, D_MODEL), const), pl.BlockSpec((D_B, D_MODEL), const),
            pl.BlockSpec((D_MODEL, D_MODEL), const), pl.BlockSpec((1, D_MODEL), const),
        ],
        out_specs=pl.BlockSpec((tm, D_MODEL), row),
        out_shape=jax.ShapeDtypeStruct((n_tok, D_MODEL), f32),
        compiler_params=pltpu.CompilerParams(
            dimension_semantics=("arbitrary",), vmem_limit_bytes=VMEM_LIMIT),
        name="merge",
    )(x2d, ya, yb, m, w_pa, w_pb, w_out, final_g)


def _pad_lora(w):
    z = jnp.zeros_like(w[0])
    return jnp.stack([jnp.concatenate([w[0], z], axis=0), jnp.concatenate([z, w[1]], axis=0)], axis=0).astype(bf16)


def _trunk(x, norm_g, w_in, shift_mu, w0, w2, a0, a2, k_k, k_a, r_k, gn_w, gn_b, rpb, w_pa, w_pb, w_out,
           final_g):
    B, T, _ = x.shape
    depth = norm_g.shape[0]
    x2d = x.reshape(B * T, D_MODEL)
    row = lambda a: a.reshape(1, -1).astype(f32)
    for l in range(depth):
        za, ga, qkv, gb, m = _inproj(x2d, row(norm_g[l]), w_in[l].astype(bf16))
        ya = _rwkv(za.reshape(B, T, A_SHIFT), ga.reshape(B, T, D_A), row(shift_mu[l]), w0[l].astype(f32),
                   _pad_lora(w2[l]), a0[l].astype(f32), _pad_lora(a2[l]), row(k_k[l]), row(k_a[l]),
                   row(r_k[l]), row(gn_w[l]), row(gn_b[l]))
        yb = _natten(qkv.reshape(B, T, 3 * D_B), gb.reshape(B, T, D_B), _natten_bias_table(rpb[l]))
        x2d = _merge(x2d, ya.reshape(B * T, D_A), yb.reshape(B * T, D_B), m, w_pa[l].astype(bf16),
                     w_pb[l].astype(bf16), w_out[l].astype(bf16), row(final_g), l == depth - 1)
    return x2d.reshape(B, T, D_MODEL)


def kernel(x_prompt, x_sample, norm_g, w_in, shift_mu, w0, w2, a0, a2, k_k, k_a, r_k, gn_w, gn_b, rpb, w_pa,
           w_pb, w_out, final_g):
    params = (norm_g, w_in, shift_mu, w0, w2, a0, a2, k_k, k_a, r_k, gn_w, gn_b, rpb, w_pa, w_pb, w_out,
              final_g)
    return (_trunk(x_prompt, *params), _trunk(x_sample, *params))
```

```python
import functools
import math

import numpy as np
import jax
import jax.numpy as jnp
from jax import lax
from jax.experimental import pallas as pl
from jax.experimental.pallas import tpu as pltpu

f32 = jnp.float32
bf16 = jnp.bfloat16

D_MODEL = 1024
GRID_W = 64
D_A = 512
HEAD = 64
LORA = 64
GN_EPS = 6.4e-4
D_B = 512
MAX_KR = 8
WIN_C = 16
RMS_EPS = 1e-6
A_SHIFT = 3 * D_A + 4 * LORA
OFF_GA = A_SHIFT
OFF_QB = OFF_GA + D_A
OFF_GB = OFF_QB + 3 * D_B
OFF_MA = OFF_GB + D_B
D_IN = OFF_MA + 2 * D_MODEL

LANES = 128
N_PAIR = D_A // LANES
CHUNK = 64
FINISH_ROWS = 256
BF16_ROWS = 16
F32_ROWS = 8
TOKEN_TILE = 512
VMEM_LIMIT = 56 * 1024 * 1024
LOG_DECAY_SCALE = -math.exp(-0.5)

_NT = (((1,), (1,)), ((), ()))


def _sigmoid(x):
    return 1.0 / (1.0 + jnp.exp(-x))


def _dot(a, b):
    return jnp.dot(a, b, preferred_element_type=f32)


def _dot_nt(a, b):
    return lax.dot_general(a, b, _NT, preferred_element_type=f32)


def _rms_norm(x, g):
    return x * lax.rsqrt(jnp.mean(x * x, axis=-1, keepdims=True) + RMS_EPS) * g


_IN_SEGMENTS = ((0, OFF_GA), (OFF_GA, OFF_QB), (OFF_QB, OFF_GB), (OFF_GB, OFF_MA), (OFF_MA, D_IN))


def _inproj_kernel(x_ref, g_ref, w_ref, *out_refs):
    h = _rms_norm(x_ref[...], g_ref[...]).astype(bf16)
    for ref, (lo, hi) in zip(out_refs, _IN_SEGMENTS):
        ref[...] = _dot(h, w_ref[:, lo:hi]).astype(bf16)


def _inproj(x2d, g, w_bf16):
    n_tok = x2d.shape[0]
    tm = TOKEN_TILE
    const = lambda i: (0, 0)
    row = lambda i: (i, 0)
    return pl.pallas_call(
        _inproj_kernel,
        grid=(n_tok // tm,),
        in_specs=[
            pl.BlockSpec((tm, D_MODEL), row),
            pl.BlockSpec((1, D_MODEL), const),
            pl.BlockSpec((D_MODEL, D_IN), const, pipeline_mode=pl.Buffered(1)),
        ],
        out_specs=[pl.BlockSpec((tm, hi - lo), row) for lo, hi in _IN_SEGMENTS],
        out_shape=[jax.ShapeDtypeStruct((n_tok, hi - lo), bf16) for lo, hi in _IN_SEGMENTS],
        compiler_params=pltpu.CompilerParams(
            dimension_semantics=("arbitrary",), vmem_limit_bytes=VMEM_LIMIT),
        name="inproj",
    )(x2d, g, w_bf16)


(_OP_R, _OP_A, _OP_A_H0, _OP_A_H1, _OP_B_H0, _OP_B_H1, _OP_K_H0, _OP_K_H1, _OP_V_H0, _OP_V_H1, _OP_V,
 _OP_BHAT, _OP_KHAT) = range(13)
N_OPND = 13


def _rwkv_kernel(za_ref, ga_ref, mu_ref, w0_ref, w2_ref, a0_ref, a2_ref, kk_ref, ka_ref, rk_ref,
                 gnw_ref, gnb_ref, out_ref, g_scr, o_scr, bonus_scr, opnd_scr, ptot_scr, *, seq_len):
    C = CHUNK
    n_chunks = seq_len // C
    window = C + 2 * BF16_ROWS
    lane = lax.broadcasted_iota(jnp.int32, (1, LANES), 1)
    head0 = lane < HEAD
    lane_all = lax.broadcasted_iota(jnp.int32, (1, D_A), 1)
    first_head = ((lane_all // HEAD) % 2 == 0)
    keep_h0 = first_head.astype(bf16)
    keep_h1 = jnp.logical_not(first_head).astype(bf16)
    t_i = lax.broadcasted_iota(jnp.int32, (C, 2 * C), 0)
    s_i = lax.broadcasted_iota(jnp.int32, (C, 2 * C), 1) % C
    strict = (t_i > s_i, t_i < s_i)
    incl = (t_i >= s_i, t_i <= s_i)
    eye_pair = (t_i == s_i).astype(f32)
    r2 = lax.broadcasted_iota(jnp.int32, (2 * C, 2 * C), 0)
    c2 = lax.broadcasted_iota(jnp.int32, (2 * C, 2 * C), 1)
    blockdiag = (r2 // C) == (c2 // C)
    seg_ones = blockdiag.astype(bf16)
    tt = lax.broadcasted_iota(jnp.int32, (C, C), 0)
    ss = lax.broadcasted_iota(jnp.int32, (C, C), 1)
    tri = ((tt >= ss).astype(bf16), (tt <= ss).astype(bf16))
    w_row = lax.broadcasted_iota(jnp.int32, (C, window), 0)
    w_col = lax.broadcasted_iota(jnp.int32, (C, window), 1)

    def stack(x):
        return jnp.concatenate([jnp.where(head0, x, 0.0), jnp.where(head0, 0.0, x)], axis=0).astype(bf16)

    def blockdiag_of(x):
        return jnp.where(blockdiag, jnp.concatenate([x, x], axis=0), 0.0).astype(bf16)

    def seg_sum(x):
        return jnp.concatenate(
            [_dot(x[:, p * LANES:(p + 1) * LANES].astype(bf16), seg_ones) for p in range(N_PAIR)], axis=1)

    def shifted_chunk(t0):
        start = pl.multiple_of(jnp.clip(t0 - BF16_ROWS, 0, seq_len - window), BF16_ROWS)
        off = t0 - start
        pick = jnp.logical_or(w_col == w_row + (off - 1), w_col == w_row + (off + 1))
        nb = _dot(pick.astype(bf16), za_ref[pl.ds(start, window), :])
        zc = za_ref[pl.ds(t0, C), :].astype(f32)
        return zc + mu_ref[...] * (0.5 * nb - zc)

    dirs = (0, 1)
    chains = [(d, p) for d in dirs for p in range(N_PAIR)]
    pair_lanes = [slice(p * LANES, (p + 1) * LANES) for p in range(N_PAIR)]

    def prepare(chunk, slot):
        t0 = [pl.multiple_of(chunk[d] * C, C) for d in dirs]
        z = [shifted_chunk(t0[d]) for d in dirs]
        r = [z[d][:, 0:D_A] for d in dirs]
        k = [z[d][:, D_A:2 * D_A] for d in dirs]
        v = [z[d][:, 2 * D_A:3 * D_A] for d in dirs]
        yield
        w_lo =[_dot(jnp.tanh(z[d][:, 3 * D_A:3 * D_A + 2 * LORA]).astype(bf16), w2_ref[d]) for d in dirs]
        a_lo = [_dot(z[d][:, 3 * D_A + 2 * LORA:A_SHIFT].astype(bf16), a2_ref[d]) for d in dirs]
        kk = [k[d] * kk_ref[...] for d in dirs]
        yield
        ld = [LOG_DECAY_SCALE * _sigmoid(w0_ref[d:d + 1, :] + w_lo[d]) for d in dirs]
        ld_hi = [ld[d].astype(bf16) for d in dirs]
        ld_lo = [(ld[d] - ld_hi[d].astype(f32)).astype(bf16) for d in dirs]
        cs = [_dot(tri[d], ld_hi[d]) + _dot(tri[d], ld_lo[d]) for d in dirs]
        yield
        a = [_sigmoid(a0_ref[d:d + 1, :] + a_lo[d]) for d in dirs]
        k_dir = [k[d] * (1.0 + (a[d] - 1.0) * ka_ref[...]) for d in dirs]
        sums = seg_sum(jnp.concatenate(
            [kk[d] * kk[d] for d in dirs] + [r[d] * k_dir[d] * rk_ref[...] for d in dirs], axis=0))
        kk = [kk[d] * lax.rsqrt(jnp.maximum(sums[d * C:(d + 1) * C], 1e-24)) for d in dirs]
        bonus_s = [sums[(2 + d) * C:(3 + d) * C] for d in dirs]
        yield
        e_pos = [jnp.exp(cs[d]) for d in dirs]
        e_neg = [jnp.exp(-cs[d]) for d in dirs]
        e_prev = [jnp.exp(cs[d] - ld[d]) for d in dirs]
        p_tot = [e_pos[0][C - 1:, :], e_pos[1][:1, :]]
        for d in dirs:
            a_t = (-kk[d] * e_prev[d]).astype(bf16)
            b_t = kk[d] * a[d] * e_neg[d]
            k_t = k_dir[d] * e_neg[d]
            v_b = v[d].astype(bf16)
            opnd_scr[slot, d, _OP_R] = (r[d] * e_pos[d]).astype(bf16)
            opnd_scr[slot, d, _OP_A] = a_t
            opnd_scr[slot, d, _OP_A_H0] = a_t * keep_h0
            opnd_scr[slot, d, _OP_A_H1] = a_t * keep_h1
            opnd_scr[slot, d, _OP_B_H0] = b_t.astype(bf16) * keep_h0
            opnd_scr[slot, d, _OP_B_H1] = b_t.astype(bf16) * keep_h1
            opnd_scr[slot, d, _OP_K_H0] = k_t.astype(bf16) * keep_h0
            opnd_scr[slot, d, _OP_K_H1] = k_t.astype(bf16) * keep_h1
            opnd_scr[slot, d, _OP_V_H0] = v_b * keep_h0
            opnd_scr[slot, d, _OP_V_H1] = v_b * keep_h1
            opnd_scr[slot, d, _OP_V] = v_b
            opnd_scr[slot, d, _OP_BHAT] = (b_t * p_tot[d]).astype(bf16)
            opnd_scr[slot, d, _OP_KHAT] = (k_t * p_tot[d]).astype(bf16)
            ptot_scr[slot, d] = jnp.broadcast_to(p_tot[d], ptot_scr.shape[2:])
            bonus_scr[d, pl.ds(t0[d], C), :] = bonus_s[d] * v[d]
        yield

    def per_chain(fn):
        return [fn(n, d, pair_lanes[p]) for n, (d, p) in enumerate(chains)]

    def rows_of(slot, d, sl, *ops):
        parts = [opnd_scr[slot, d, j, :, sl] for j in ops]
        return parts[0] if len(parts) == 1 else jnp.concatenate(parts, axis=0)

    def chain_stages(slot, t0):
        s = per_chain(lambda n, d, sl: _dot_nt(
            rows_of(slot, d, sl, _OP_A, _OP_R),
            rows_of(slot, d, sl, _OP_B_H0, _OP_B_H1, _OP_K_H0, _OP_K_H1)))
        yield
        a_ab = per_chain(lambda n, d, sl: jnp.where(strict[d], s[n][:C, :2 * C], 0.0))
        a_xk = per_chain(lambda n, d, sl: jnp.concatenate(
            [jnp.where(strict[d], s[n][:C, 2 * C:], 0.0), jnp.where(incl[d], s[n][C:, 2 * C:], 0.0)],
            axis=0).astype(bf16))
        a_rb = per_chain(lambda n, d, sl: jnp.where(incl[d], s[n][C:, :2 * C], 0.0).astype(bf16))
        xv = per_chain(lambda n, d, sl: _dot(a_xk[n], rows_of(slot, d, sl, _OP_V_H0, _OP_V_H1)))
        a_pow = per_chain(lambda n, d, sl: _dot(a_ab[n].astype(bf16), blockdiag_of(a_ab[n])))
        t_inv = per_chain(lambda n, d, sl: eye_pair + a_ab[n])
        yield
        n_sq = 2
        while n_sq < C:
            last = 2 * n_sq >= C
            lhs = per_chain(lambda n, d, sl: (t_inv[n] if last else jnp.concatenate(
                [t_inv[n], a_pow[n]], axis=0)).astype(bf16))
            prod = per_chain(lambda n, d, sl: _dot(lhs[n], blockdiag_of(a_pow[n])))
            t_inv = per_chain(lambda n, d, sl: t_inv[n] + prod[n][:C])
            if not last:
                a_pow = per_chain(lambda n, d, sl: prod[n][C:])
            n_sq *= 2
            yield
        wu = per_chain(lambda n, d, sl: _dot(t_inv[n].astype(bf16), jnp.concatenate(
            [rows_of(slot, d, sl, _OP_A_H0, _OP_A_H1), stack(xv[n][:C])], axis=1)))
        yield
        g = [g_scr[d, p] for d, p in chains]
        wr_g = per_chain(lambda n, d, sl: _dot_nt(
            jnp.concatenate([wu[n][:, :LANES].astype(bf16), rows_of(slot, d, sl, _OP_R)], axis=0),
            g[n].astype(bf16)))
        u = per_chain(lambda n, d, sl: wr_g[n][:C] + wu[n][:, LANES:])
        yield
        o = per_chain(lambda n, d, sl: wr_g[n][C:] + _dot(a_rb[n], stack(u[n])) + xv[n][C:])
        upd = per_chain(lambda n, d, sl: _dot(
            jnp.concatenate([u[n], rows_of(slot, d, sl, _OP_V).astype(f32)], axis=0).T.astype(bf16),
            rows_of(slot, d, sl, _OP_BHAT, _OP_KHAT)))
        for n, (d, p) in enumerate(chains):
            g_scr[d, p] = g[n] * ptot_scr[slot, d, 0:1, pair_lanes[p]] + jnp.where(blockdiag, upd[n], 0.0)
        for d in dirs:
            o_scr[d, pl.ds(t0[d], C), :] = jnp.concatenate(o[d * N_PAIR:(d + 1) * N_PAIR], axis=1)
        yield

    def trip_chunks(j):
        return (jnp.minimum(j, n_chunks - 1), jnp.maximum(n_chunks - 1 - j, 0))

    def step(i, carry):
        cur = lax.rem(i, 2)
        chunk = trip_chunks(i)
        work = chain_stages(cur, [pl.multiple_of(chunk[d] * C, C) for d in dirs])
        prep = prepare(trip_chunks(i + 1), 1 - cur)
        next(work)
        next(prep)
        next(work)
        next(work)
        next(prep)
        next(work)
        next(work)
        next(prep)
        next(work)
        next(work)
        next(prep)
        next(work)
        next(work)
        next(work)
        next(prep)
        return carry

    g_scr[...] = jnp.zeros_like(g_scr)
    for _ in prepare(trip_chunks(0), 0):
        pass
    lax.fori_loop(0, n_chunks, step, 0)

    def finish(i, carry):
        rows = pl.ds(pl.multiple_of(i * FINISH_ROWS, FINISH_ROWS), FINISH_ROWS)
        o = o_scr[0, rows, :] + o_scr[1, rows, :]
        mean = seg_sum(o) * (1.0 / HEAD)
        oc = o - mean
        var = seg_sum(oc * oc) * (1.0 / HEAD)
        y = oc * lax.rsqrt(var + GN_EPS) * gnw_ref[...] + gnb_ref[...]
        y = y + bonus_scr[0, rows, :] + bonus_scr[1, rows, :]
        gate = ga_ref[rows, :].astype(f32)
        out_ref[rows, :] = (y * (gate * _sigmoid(gate))).astype(bf16)
        return carry

    lax.fori_loop(0, seq_len // FINISH_ROWS, finish, 0)


def _rwkv(za, ga, mu, w0, w2p, a0, a2p, k_k, k_a, r_k, gn_w, gn_b):
    B, T, _ = za.shape
    seq = lambda width: pl.BlockSpec((None, T, width), lambda b: (b, 0, 0))
    full = lambda *shape: pl.BlockSpec(shape, lambda b: (0,) * len(shape))
    return pl.pallas_call(
        functools.partial(_rwkv_kernel, seq_len=T),
        grid=(B,),
        in_specs=[
            seq(A_SHIFT), seq(D_A), full(1, A_SHIFT), full(2, D_A), full(2, 2 * LORA, D_A),
            full(2, D_A), full(2, 2 * LORA, D_A), full(1, D_A), full(1, D_A), full(1, D_A),
            full(1, D_A), full(1, D_A),
        ],
        out_specs=seq(D_A),
        out_shape=jax.ShapeDtypeStruct((B, T, D_A), bf16),
        scratch_shapes=[
            pltpu.VMEM((2, N_PAIR, LANES, LANES), f32),
            pltpu.VMEM((2, T, D_A), f32),
            pltpu.VMEM((2, T, D_A), f32),
            pltpu.VMEM((2, 2, N_OPND, CHUNK, D_A), bf16),
            pltpu.VMEM((2, 2, F32_ROWS, D_A), f32),
        ],
        compiler_params=pltpu.CompilerParams(
            dimension_semantics=("arbitrary",), vmem_limit_bytes=VMEM_LIMIT),
        name="rwkv7",
    )(za, ga, mu, w0, w2p, a0, a2p, k_k, k_a, r_k, gn_w, gn_b)


def _natten_bias_table(rpb):
    qc = np.arange(GRID_W)
    kc = np.arange(GRID_W)
    cs = np.clip(qc - WIN_C // 2, 0, GRID_W - WIN_C)
    col_mask = (kc[None, :] >= cs[:, None]) & (kc[None, :] < cs[:, None] + WIN_C)
    dc = np.clip(kc[None, :] - qc[:, None] + WIN_C - 1, 0, 2 * WIN_C - 2)
    tbl = jnp.where(col_mask[None, None], rpb.astype(f32)[:, :, dc], -1e30)
    variants = [tbl[:, d0:d0 + MAX_KR].transpose(0, 2, 1, 3).reshape(rpb.shape[0] // 2, 2 * GRID_W, MAX_KR * GRID_W)
                for d0 in range(MAX_KR)]
    return jnp.stack(variants, axis=0)


def _natten_kernel(qkv_ref, gb_ref, tbl_ref, out_ref, *, seq_len):
    n_rows = seq_len // GRID_W
    n_keys = MAX_KR * GRID_W
    lane = lax.broadcasted_iota(jnp.int32, (1, LANES), 1)
    head0 = lane < HEAD

    def row_step(r, carry):
        rs = jnp.clip(r - MAX_KR // 2, 0, n_rows - MAX_KR)
        d0 = rs - r + (MAX_KR - 1)
        q_rows = pl.ds(pl.multiple_of(r * GRID_W, GRID_W), GRID_W)
        k_rows = pl.ds(pl.multiple_of(rs * GRID_W, GRID_W), n_keys)
        pairs = range(N_PAIR)
        lanes = [slice(p * LANES, (p + 1) * LANES) for p in pairs]
        q = [qkv_ref[q_rows, lanes[p]] * (HEAD ** -0.5) for p in pairs]
        q2 = [jnp.concatenate([jnp.where(head0, x, jnp.zeros_like(x)), jnp.where(head0, jnp.zeros_like(x), x)],
                              axis=0) for x in q]
        s = [_dot_nt(q2[p], qkv_ref[k_rows, D_B + p * LANES:D_B + (p + 1) * LANES]) + tbl_ref[d0, p] for p in pairs]
        e = [jnp.exp(x - jnp.max(x, axis=-1, keepdims=True)) for x in s]
        inv_den = [1.0 / jnp.sum(x, axis=-1, keepdims=True) for x in e]
        o = [_dot(e[p].astype(bf16), qkv_ref[k_rows, 2 * D_B + p * LANES:2 * D_B + (p + 1) * LANES]) * inv_den[p]
             for p in pairs]
        for p in pairs:
            gate = gb_ref[q_rows, lanes[p]].astype(f32)
            pair_out = jnp.where(head0, o[p][:GRID_W], o[p][GRID_W:])
            out_ref[q_rows, lanes[p]] = (pair_out * (gate * _sigmoid(gate))).astype(bf16)
        return carry

    lax.fori_loop(0, n_rows, row_step, 0)


def _natten(qkv, gb, tbl):
    B, T, _ = qkv.shape
    seq = lambda width: pl.BlockSpec((None, T, width), lambda b: (b, 0, 0))
    return pl.pallas_call(
        functools.partial(_natten_kernel, seq_len=T),
        grid=(B,),
        in_specs=[seq(3 * D_B), seq(D_B),
                  pl.BlockSpec(tbl.shape, lambda b: (0, 0, 0, 0), pipeline_mode=pl.Buffered(1))],
        out_specs=seq(D_B),
        out_shape=jax.ShapeDtypeStruct((B, T, D_B), bf16),
        compiler_params=pltpu.CompilerParams(
            dimension_semantics=("arbitrary",), vmem_limit_bytes=VMEM_LIMIT),
        name="natten",
    )(qkv, gb, tbl)


def _merge_kernel(x_ref, ya_ref, yb_ref, m_ref, wpa_ref, wpb_ref, wout_ref, fg_ref, out_ref, *, final_norm):
    m = m_ref[...].astype(f32)
    merged = (_sigmoid(m[:, :D_MODEL]) * _dot(ya_ref[...], wpa_ref[...])
              + _sigmoid(m[:, D_MODEL:]) * _dot(yb_ref[...], wpb_ref[...]))
    y = x_ref[...] + _dot(merged.astype(bf16), wout_ref[...])
    if final_norm:
        y = _rms_norm(y, fg_ref[...])
    out_ref[...] = y


def _merge(x2d, ya, yb, m, w_pa, w_pb, w_out, final_g, final_norm):
    n_tok = x2d.shape[0]
    tm = TOKEN_TILE
    const = lambda i: (0, 0)
    row = lambda i: (i, 0)
    return pl.pallas_call(
        functools.partial(_merge_kernel, final_norm=final_norm),
        grid=(n_tok // tm,),
        in_specs=[
            pl.BlockSpec((tm, D_MODEL), row), pl.BlockSpec((tm, D_A), row), pl.BlockSpec((tm, D_B), row),
            pl.BlockSpec((tm, 2 * D_MODEL), row),
            pl.BlockSpec((D_A, D_MODEL), const), pl.BlockSpec((D_B, D_MODEL), const),
            pl.BlockSpec((D_MODEL, D_MODEL), const), pl.BlockSpec((1, D_MODEL), const),
        ],
        out_specs=pl.BlockSpec((tm, D_MODEL), row),
        out_shape=jax.ShapeDtypeStruct((n_tok, D_MODEL), f32),
        compiler_params=pltpu.CompilerParams(
            dimension_semantics=("arbitrary",), vmem_limit_bytes=VMEM_LIMIT),
        name="merge",
    )(x2d, ya, yb, m, w_pa, w_pb, w_out, final_g)


def _pad_lora(w):
    z = jnp.zeros_like(w[0])
    return jnp.stack([jnp.concatenate([w[0], z], axis=0), jnp.concatenate([z, w[1]], axis=0)], axis=0).astype(bf16)


def _trunk(x, norm_g, w_in, shift_mu, w0, w2, a0, a2, k_k, k_a, r_k, gn_w, gn_b, rpb, w_pa, w_pb, w_out,
           final_g):
    B, T, _ = x.shape
    depth = norm_g.shape[0]
    x2d = x.reshape(B * T, D_MODEL)
    row = lambda a: a.reshape(1, -1).astype(f32)
    for l in range(depth):
        za, ga, qkv, gb, m = _inproj(x2d, row(norm_g[l]), w_in[l].astype(bf16))
        ya = _rwkv(za.reshape(B, T, A_SHIFT), ga.reshape(B, T, D_A), row(shift_mu[l]), w0[l].astype(f32),
                   _pad_lora(w2[l]), a0[l].astype(f32), _pad_lora(a2[l]), row(k_k[l]), row(k_a[l]),
                   row(r_k[l]), row(gn_w[l]), row(gn_b[l]))
        yb = _natten(qkv.reshape(B, T, 3 * D_B), gb.reshape(B, T, D_B), _natten_bias_table(rpb[l]))
        x2d = _merge(x2d, ya.reshape(B * T, D_A), yb.reshape(B * T, D_B), m, w_pa[l].astype(bf16),
                     w_pb[l].astype(bf16), w_out[l].astype(bf16), row(final_g), l == depth - 1)
    return x2d.reshape(B, T, D_MODEL)


def kernel(x_prompt, x_sample, norm_g, w_in, shift_mu, w0, w2, a0, a2, k_k, k_a, r_k, gn_w, gn_b, rpb, w_pa,
           w_pb, w_out, final_g):
    params = (norm_g, w_in, shift_mu, w0, w2, a0, a2, k_k, k_a, r_k, gn_w, gn_b, rpb, w_pa, w_pb, w_out,
              final_g)
    return (_trunk(x_prompt, *params), _trunk(x_sample, *params))
```

```python
import functools
import math

import numpy as np
import jax
import jax.numpy as jnp
from jax import lax
from jax.experimental import pallas as pl
from jax.experimental.pallas import tpu as pltpu

f32 = jnp.float32
bf16 = jnp.bfloat16

D_MODEL = 1024
GRID_W = 64
D_A = 512
HEAD = 64
LORA = 64
GN_EPS = 6.4e-4
D_B = 512
MAX_KR = 8
WIN_C = 16
RMS_EPS = 1e-6
A_SHIFT = 3 * D_A + 4 * LORA
OFF_GA = A_SHIFT
OFF_QB = OFF_GA + D_A
OFF_GB = OFF_QB + 3 * D_B
OFF_MA = OFF_GB + D_B
D_IN = OFF_MA + 2 * D_MODEL

LANES = 128
N_PAIR = D_A // LANES
CHUNK = 64
FINISH_ROWS = 256
BF16_ROWS = 16
F32_ROWS = 8
TOKEN_TILE = 512
VMEM_LIMIT = 56 * 1024 * 1024
LOG_DECAY_SCALE = -math.exp(-0.5)

_NT = (((1,), (1,)), ((), ()))


def _sigmoid(x):
    return 1.0 / (1.0 + jnp.exp(-x))


def _dot(a, b):
    return jnp.dot(a, b, preferred_element_type=f32)


def _dot_nt(a, b):
    return lax.dot_general(a, b, _NT, preferred_element_type=f32)


def _rms_norm(x, g):
    return x * lax.rsqrt(jnp.mean(x * x, axis=-1, keepdims=True) + RMS_EPS) * g


_IN_SEGMENTS = ((0, OFF_GA), (OFF_GA, OFF_QB), (OFF_QB, OFF_GB), (OFF_GB, OFF_MA), (OFF_MA, D_IN))


def _inproj_kernel(x_ref, g_ref, w_ref, *out_refs):
    h = _rms_norm(x_ref[...], g_ref[...]).astype(bf16)
    for ref, (lo, hi) in zip(out_refs, _IN_SEGMENTS):
        ref[...] = _dot(h, w_ref[:, lo:hi]).astype(bf16)


def _inproj(x2d, g, w_bf16):
    n_tok = x2d.shape[0]
    tm = TOKEN_TILE
    const = lambda i: (0, 0)
    row = lambda i: (i, 0)
    return pl.pallas_call(
        _inproj_kernel,
        grid=(n_tok // tm,),
        in_specs=[
            pl.BlockSpec((tm, D_MODEL), row),
            pl.BlockSpec((1, D_MODEL), const),
            pl.BlockSpec((D_MODEL, D_IN), const, pipeline_mode=pl.Buffered(1)),
        ],
        out_specs=[pl.BlockSpec((tm, hi - lo), row) for lo, hi in _IN_SEGMENTS],
        out_shape=[jax.ShapeDtypeStruct((n_tok, hi - lo), bf16) for lo, hi in _IN_SEGMENTS],
        compiler_params=pltpu.CompilerParams(
            dimension_semantics=("arbitrary",), vmem_limit_bytes=VMEM_LIMIT),
        name="inproj",
    )(x2d, g, w_bf16)


(_OP_R, _OP_A, _OP_A_H0, _OP_A_H1, _OP_V_H0, _OP_V_H1, _OP_V) = range(7)
N_OPND = 7


def _rwkv_kernel(za_ref, ga_ref, mu_ref, w0_ref, w2_ref, a0_ref, a2_ref, kk_ref, ka_ref, rk_ref,
                 gnw_ref, gnb_ref, out_ref, h_scr, o_scr, bonus_scr, opnd_scr, yt_scr, bkt_scr, pcol_scr, *,
                 seq_len):
    C = CHUNK
    n_chunks = seq_len // C
    window = C + 2 * BF16_ROWS
    lane = lax.broadcasted_iota(jnp.int32, (1, LANES), 1)
    head0 = lane < HEAD
    lane_all = lax.broadcasted_iota(jnp.int32, (1, D_A), 1)
    first_head = ((lane_all // HEAD) % 2 == 0)
    keep_h0 = first_head.astype(bf16)
    keep_h1 = jnp.logical_not(first_head).astype(bf16)
    t_i = lax.broadcasted_iota(jnp.int32, (C, 2 * C), 0)
    s_i = lax.broadcasted_iota(jnp.int32, (C, 2 * C), 1) % C
    strict = (t_i > s_i, t_i < s_i)
    incl = (t_i >= s_i, t_i <= s_i)
    eye_pair = (t_i == s_i).astype(f32)
    r2 = lax.broadcasted_iota(jnp.int32, (2 * C, 2 * C), 0)
    c2 = lax.broadcasted_iota(jnp.int32, (2 * C, 2 * C), 1)
    blockdiag = (r2 // C) == (c2 // C)
    seg_ones = blockdiag.astype(bf16)
    tt = lax.broadcasted_iota(jnp.int32, (C, C), 0)
    ss = lax.broadcasted_iota(jnp.int32, (C, C), 1)
    tri = ((tt >= ss).astype(bf16), (tt <= ss).astype(bf16))
    w_row = lax.broadcasted_iota(jnp.int32, (C, window), 0)
    w_col = lax.broadcasted_iota(jnp.int32, (C, window), 1)

    def stack(x):
        return jnp.concatenate([jnp.where(head0, x, 0.0), jnp.where(head0, 0.0, x)], axis=0).astype(bf16)

    def blockdiag_of(x):
        return jnp.where(blockdiag, jnp.concatenate([x, x], axis=0), 0.0).astype(bf16)

    def seg_sum(x):
        return jnp.concatenate(
            [_dot(x[:, p * LANES:(p + 1) * LANES].astype(bf16), seg_ones) for p in range(N_PAIR)], axis=1)

    def shifted_chunk(t0):
        start = pl.multiple_of(jnp.clip(t0 - BF16_ROWS, 0, seq_len - window), BF16_ROWS)
        off = t0 - start
        pick = jnp.logical_or(w_col == w_row + (off - 1), w_col == w_row + (off + 1))
        nb = _dot(pick.astype(bf16), za_ref[pl.ds(start, window), :])
        zc = za_ref[pl.ds(t0, C), :].astype(f32)
        return zc + mu_ref[...] * (0.5 * nb - zc)

    dirs = (0, 1)
    chains = [(d, p) for d in dirs for p in range(N_PAIR)]
    pair_lanes = [slice(p * LANES, (p + 1) * LANES) for p in range(N_PAIR)]

    def prepare(chunk, slot):
        t0 = [pl.multiple_of(chunk[d] * C, C) for d in dirs]
        z = [shifted_chunk(t0[d]) for d in dirs]
        r = [z[d][:, 0:D_A] for d in dirs]
        k = [z[d][:, D_A:2 * D_A] for d in dirs]
        v = [z[d][:, 2 * D_A:3 * D_A] for d in dirs]
        yield
        w_lo =[_dot(jnp.tanh(z[d][:, 3 * D_A:3 * D_A + 2 * LORA]).astype(bf16), w2_ref[d]) for d in dirs]
        a_lo = [_dot(z[d][:, 3 * D_A + 2 * LORA:A_SHIFT].astype(bf16), a2_ref[d]) for d in dirs]
        kk = [k[d] * kk_ref[...] for d in dirs]
        yield
        ld = [LOG_DECAY_SCALE * _sigmoid(w0_ref[d:d + 1, :] + w_lo[d]) for d in dirs]
        ld_hi = [ld[d].astype(bf16) for d in dirs]
        ld_lo = [(ld[d] - ld_hi[d].astype(f32)).astype(bf16) for d in dirs]
        cs = [_dot(tri[d], ld_hi[d]) + _dot(tri[d], ld_lo[d]) for d in dirs]
        yield
        a = [_sigmoid(a0_ref[d:d + 1, :] + a_lo[d]) for d in dirs]
        k_dir = [k[d] * (1.0 + (a[d] - 1.0) * ka_ref[...]) for d in dirs]
        sums = seg_sum(jnp.concatenate(
            [kk[d] * kk[d] for d in dirs] + [r[d] * k_dir[d] * rk_ref[...] for d in dirs], axis=0))
        kk = [kk[d] * lax.rsqrt(jnp.maximum(sums[d * C:(d + 1) * C], 1e-24)) for d in dirs]
        bonus_s = [sums[(2 + d) * C:(3 + d) * C] for d in dirs]
        yield
        e_pos = [jnp.exp(cs[d]) for d in dirs]
        e_neg = [jnp.exp(-cs[d]) for d in dirs]
        e_prev = [jnp.exp(cs[d] - ld[d]) for d in dirs]
        p_tot = [e_pos[0][C - 1:, :], e_pos[1][:1, :]]
        for d in dirs:
            a_t = (-kk[d] * e_prev[d]).astype(bf16)
            b_t = kk[d] * a[d] * e_neg[d]
            k_t = k_dir[d] * e_neg[d]
            b_hat = b_t * p_tot[d]
            k_hat = k_t * p_tot[d]
            v_b = v[d].astype(bf16)
            opnd_scr[slot, d, _OP_R] = (r[d] * e_pos[d]).astype(bf16)
            opnd_scr[slot, d, _OP_A] = a_t
            opnd_scr[slot, d, _OP_A_H0] = a_t * keep_h0
            opnd_scr[slot, d, _OP_A_H1] = a_t * keep_h1
            opnd_scr[slot, d, _OP_V_H0] = v_b * keep_h0
            opnd_scr[slot, d, _OP_V_H1] = v_b * keep_h1
            opnd_scr[slot, d, _OP_V] = v_b
            bonus_scr[d, pl.ds(t0[d], C), :] = bonus_s[d] * v[d]
            for p, sl in enumerate(pair_lanes):
                n = d * N_PAIR + p
                b_p, k_p = b_t[:, sl], k_t[:, sl]
                y_all = jnp.concatenate([jnp.where(head0, b_p, 0.0), jnp.where(head0, 0.0, b_p),
                                         jnp.where(head0, k_p, 0.0), jnp.where(head0, 0.0, k_p)], axis=0)
                yt_scr[slot, n] = y_all.T.astype(bf16)
                bkt_scr[slot, n] = jnp.concatenate([b_hat[:, sl], k_hat[:, sl]], axis=0).T.astype(bf16)
                pcol_scr[slot, n] = jnp.broadcast_to(p_tot[d][:, sl], (LANES, LANES)).T
        yield

    def per_chain(fn):
        return [fn(n, d, pair_lanes[p]) for n, (d, p) in enumerate(chains)]

    def rows_of(slot, d, sl, *ops):
        parts = [opnd_scr[slot, d, j, :, sl] for j in ops]
        return parts[0] if len(parts) == 1 else jnp.concatenate(parts, axis=0)

    def chain_stages(slot, t0):
        s = per_chain(lambda n, d, sl: _dot(rows_of(slot, d, sl, _OP_A, _OP_R), yt_scr[slot, n]))
        yield
        a_ab = per_chain(lambda n, d, sl: jnp.where(strict[d], s[n][:C, :2 * C], 0.0))
        a_xk = per_chain(lambda n, d, sl: jnp.concatenate(
            [jnp.where(strict[d], s[n][:C, 2 * C:], 0.0), jnp.where(incl[d], s[n][C:, 2 * C:], 0.0)],
            axis=0).astype(bf16))
        a_rb = per_chain(lambda n, d, sl: jnp.where(incl[d], s[n][C:, :2 * C], 0.0).astype(bf16))
        xv = per_chain(lambda n, d, sl: _dot(a_xk[n], rows_of(slot, d, sl, _OP_V_H0, _OP_V_H1)))
        a_pow = per_chain(lambda n, d, sl: _dot(a_ab[n].astype(bf16), blockdiag_of(a_ab[n])))
        t_inv = per_chain(lambda n, d, sl: eye_pair + a_ab[n])
        yield
        n_sq = 2
        while n_sq < C:
            last = 2 * n_sq >= C
            lhs = per_chain(lambda n, d, sl: (t_inv[n] if last else jnp.concatenate(
                [t_inv[n], a_pow[n]], axis=0)).astype(bf16))
            prod = per_chain(lambda n, d, sl: _dot(lhs[n], blockdiag_of(a_pow[n])))
            t_inv = per_chain(lambda n, d, sl: t_inv[n] + prod[n][:C])
            if not last:
                a_pow = per_chain(lambda n, d, sl: prod[n][C:])
            n_sq *= 2
            yield
        wu = per_chain(lambda n, d, sl: _dot(t_inv[n].astype(bf16), jnp.concatenate(
            [rows_of(slot, d, sl, _OP_A_H0, _OP_A_H1), stack(xv[n][:C])], axis=1)))
        yield
        h = [h_scr[n] for n in range(len(chains))]
        wr_h = per_chain(lambda n, d, sl: _dot(
            jnp.concatenate([wu[n][:, :LANES].astype(bf16), rows_of(slot, d, sl, _OP_R)], axis=0),
            h[n].astype(bf16)))
        u = per_chain(lambda n, d, sl: wr_h[n][:C] + wu[n][:, LANES:])
        yield
        o = per_chain(lambda n, d, sl: wr_h[n][C:] + _dot(a_rb[n], stack(u[n])) + xv[n][C:])
        upd = per_chain(lambda n, d, sl: _dot(
            bkt_scr[slot, n], jnp.concatenate([u[n].astype(bf16), rows_of(slot, d, sl, _OP_V)], axis=0)))
        for n in range(len(chains)):
            h_scr[n] = h[n] * pcol_scr[slot, n] + jnp.where(blockdiag, upd[n], 0.0)
        for d in dirs:
            o_scr[d, pl.ds(t0[d], C), :] = jnp.concatenate(o[d * N_PAIR:(d + 1) * N_PAIR], axis=1)
        yield

    def trip_chunks(j):
        return (jnp.minimum(j, n_chunks - 1), jnp.maximum(n_chunks - 1 - j, 0))

    def step(i, carry):
        cur = lax.rem(i, 2)
        chunk = trip_chunks(i)
        work = chain_stages(cur, [pl.multiple_of(chunk[d] * C, C) for d in dirs])
        prep = prepare(trip_chunks(i + 1), 1 - cur)
        next(work)
        next(prep)
        next(work)
        next(work)
        next(prep)
        next(work)
        next(work)
        next(prep)
        next(work)
        next(work)
        next(prep)
        next(work)
        next(work)
        next(work)
        next(prep)
        return carry

    h_scr[...] = jnp.zeros_like(h_scr)
    for _ in prepare(trip_chunks(0), 0):
        pass
    lax.fori_loop(0, n_chunks, step, 0)

    def finish(i, carry):
        rows = pl.ds(pl.multiple_of(i * FINISH_ROWS, FINISH_ROWS), FINISH_ROWS)
        o = o_scr[0, rows, :] + o_scr[1, rows, :]
        mean = seg_sum(o) * (1.0 / HEAD)
        oc = o - mean
        var = seg_sum(oc * oc) * (1.0 / HEAD)
        y = oc * lax.rsqrt(var + GN_EPS) * gnw_ref[...] + gnb_ref[...]
        y = y + bonus_scr[0, rows, :] + bonus_scr[1, rows, :]
        gate = ga_ref[rows, :].astype(f32)
        out_ref[rows, :] = (y * (gate * _sigmoid(gate))).astype(bf16)
        return carry

    lax.fori_loop(0, seq_len // FINISH_ROWS, finish, 0)


def _rwkv(za, ga, mu, w0, w2p, a0, a2p, k_k, k_a, r_k, gn_w, gn_b):
    B, T, _ = za.shape
    seq = lambda width: pl.BlockSpec((None, T, width), lambda b: (b, 0, 0))
    full = lambda *shape: pl.BlockSpec(shape, lambda b: (0,) * len(shape))
    return pl.pallas_call(
        functools.partial(_rwkv_kernel, seq_len=T),
        grid=(B,),
        in_specs=[
            seq(A_SHIFT), seq(D_A), full(1, A_SHIFT), full(2, D_A), full(2, 2 * LORA, D_A),
            full(2, D_A), full(2, 2 * LORA, D_A), full(1, D_A), full(1, D_A), full(1, D_A),
            full(1, D_A), full(1, D_A),
        ],
        out_specs=seq(D_A),
        out_shape=jax.ShapeDtypeStruct((B, T, D_A), bf16),
        scratch_shapes=[
            pltpu.VMEM((2 * N_PAIR, LANES, LANES), f32),
            pltpu.VMEM((2, T, D_A), f32),
            pltpu.VMEM((2, T, D_A), f32),
            pltpu.VMEM((2, 2, N_OPND, CHUNK, D_A), bf16),
            pltpu.VMEM((2, 2 * N_PAIR, LANES, 4 * CHUNK), bf16),
            pltpu.VMEM((2, 2 * N_PAIR, LANES, 2 * CHUNK), bf16),
            pltpu.VMEM((2, 2 * N_PAIR, LANES, LANES), f32),
        ],
        compiler_params=pltpu.CompilerParams(
            dimension_semantics=("arbitrary",), vmem_limit_bytes=VMEM_LIMIT),
        name="rwkv7",
    )(za, ga, mu, w0, w2p, a0, a2p, k_k, k_a, r_k, gn_w, gn_b)


def _natten_bias_table(rpb):
    qc = np.arange(GRID_W)
    kc = np.arange(GRID_W)
    cs = np.clip(qc - WIN_C // 2, 0, GRID_W - WIN_C)
    col_mask = (kc[None, :] >= cs[:, None]) & (kc[None, :] < cs[:, None] + WIN_C)
    dc = np.clip(kc[None, :] - qc[:, None] + WIN_C - 1, 0, 2 * WIN_C - 2)
    tbl = jnp.where(col_mask[None, None], rpb.astype(f32)[:, :, dc], -1e30)
    variants = [tbl[:, d0:d0 + MAX_KR].transpose(0, 2, 1, 3).reshape(rpb.shape[0] // 2, 2 * GRID_W, MAX_KR * GRID_W)
                for d0 in range(MAX_KR)]
    return jnp.stack(variants, axis=0)


def _natten_kernel(qkv_ref, gb_ref, tbl_ref, out_ref, *, seq_len):
    n_rows = seq_len // GRID_W
    n_keys = MAX_KR * GRID_W
    lane = lax.broadcasted_iota(jnp.int32, (1, LANES), 1)
    head0 = lane < HEAD

    def row_step(r, carry):
        rs = jnp.clip(r - MAX_KR // 2, 0, n_rows - MAX_KR)
        d0 = rs - r + (MAX_KR - 1)
        q_rows = pl.ds(pl.multiple_of(r * GRID_W, GRID_W), GRID_W)
        k_rows = pl.ds(pl.multiple_of(rs * GRID_W, GRID_W), n_keys)
        pairs = range(N_PAIR)
        lanes = [slice(p * LANES, (p + 1) * LANES) for p in pairs]
        q = [qkv_ref[q_rows, lanes[p]] * (HEAD ** -0.5) for p in pairs]
        q2 = [jnp.concatenate([jnp.where(head0, x, jnp.zeros_like(x)), jnp.where(head0, jnp.zeros_like(x), x)],
                              axis=0) for x in q]
        s = [_dot_nt(q2[p], qkv_ref[k_rows, D_B + p * LANES:D_B + (p + 1) * LANES]) + tbl_ref[d0, p] for p in pairs]
        e = [jnp.exp(x - jnp.max(x, axis=-1, keepdims=True)) for x in s]
        inv_den = [1.0 / jnp.sum(x, axis=-1, keepdims=True) for x in e]
        o = [_dot(e[p].astype(bf16), qkv_ref[k_rows, 2 * D_B + p * LANES:2 * D_B + (p + 1) * LANES]) * inv_den[p]
             for p in pairs]
        for p in pairs:
            gate = gb_ref[q_rows, lanes[p]].astype(f32)
            pair_out = jnp.where(head0, o[p][:GRID_W], o[p][GRID_W:])
            out_ref[q_rows, lanes[p]] = (pair_out * (gate * _sigmoid(gate))).astype(bf16)
        return carry

    lax.fori_loop(0, n_rows, row_step, 0)


def _natten(qkv, gb, tbl):
    B, T, _ = qkv.shape
    seq = lambda width: pl.BlockSpec((None, T, width), lambda b: (b, 0, 0))
    return pl.pallas_call(
        functools.partial(_natten_kernel, seq_len=T),
        grid=(B,),
        in_specs=[seq(3 * D_B), seq(D_B),
                  pl.BlockSpec(tbl.shape, lambda b: (0, 0, 0, 0), pipeline_mode=pl.Buffered(1))],
        out_specs=seq(D_B),
        out_shape=jax.ShapeDtypeStruct((B, T, D_B), bf16),
        compiler_params=pltpu.CompilerParams(
            dimension_semantics=("arbitrary",), vmem_limit_bytes=VMEM_LIMIT),
        name="natten",
    )(qkv, gb, tbl)


def _merge_kernel(x_ref, ya_ref, yb_ref, m_ref, wpa_ref, wpb_ref, wout_ref, fg_ref, out_ref, *, final_norm):
    m = m_ref[...].astype(f32)
    merged = (_sigmoid(m[:, :D_MODEL]) * _dot(ya_ref[...], wpa_ref[...])
              + _sigmoid(m[:, D_MODEL:]) * _dot(yb_ref[...], wpb_ref[...]))
    y = x_ref[...] + _dot(merged.astype(bf16), wout_ref[...])
    if final_norm:
        y = _rms_norm(y, fg_ref[...])
    out_ref[...] = y


def _merge(x2d, ya, yb, m, w_pa, w_pb, w_out, final_g, final_norm):
    n_tok = x2d.shape[0]
    tm = TOKEN_TILE
    const = lambda i: (0, 0)
    row = lambda i: (i, 0)
    return pl.pallas_call(
        functools.partial(_merge_kernel, final_norm=final_norm),
        grid=(n_tok // tm,),
        in_specs=[
            pl.BlockSpec((tm, D_MODEL), row), pl.BlockSpec((tm, D_A), row), pl.BlockSpec((tm, D_B), row),
            pl.BlockSpec((tm, 2 * D_MODEL), row),
            pl.BlockSpec((D_A, D_MODEL), const), pl.BlockSpec((D_B, D_MODEL), const),
            pl.BlockSpec((D_MODEL, D_MODEL), const), pl.BlockSpec((1, D_MODEL), const),
        ],
        out_specs=pl.BlockSpec((tm, D_MODEL), row),
        out_shape=jax.ShapeDtypeStruct((n_tok, D_MODEL), f32),
        compiler_params=pltpu.CompilerParams(
            dimension_semantics=("arbitrary",), vmem_limit_bytes=VMEM_LIMIT),
        name="merge",
    )(x2d, ya, yb, m, w_pa, w_pb, w_out, final_g)


def _pad_lora(w):
    z = jnp.zeros_like(w[0])
    return jnp.stack([jnp.concatenate([w[0], z], axis=0), jnp.concatenate([z, w[1]], axis=0)], axis=0).astype(bf16)


def _trunk(x, norm_g, w_in, shift_mu, w0, w2, a0, a2, k_k, k_a, r_k, gn_w, gn_b, rpb, w_pa, w_pb, w_out,
           final_g):
    B, T, _ = x.shape
    depth = norm_g.shape[0]
    x2d = x.reshape(B * T, D_MODEL)
    row = lambda a: a.reshape(1, -1).astype(f32)
    for l in range(depth):
        za, ga, qkv, gb, m = _inproj(x2d, row(norm_g[l]), w_in[l].astype(bf16))
        ya = _rwkv(za.reshape(B, T, A_SHIFT), ga.reshape(B, T, D_A), row(shift_mu[l]), w0[l].astype(f32),
                   _pad_lora(w2[l]), a0[l].astype(f32), _pad_lora(a2[l]), row(k_k[l]), row(k_a[l]),
                   row(r_k[l]), row(gn_w[l]), row(gn_b[l]))
        yb = _natten(qkv.reshape(B, T, 3 * D_B), gb.reshape(B, T, D_B), _natten_bias_table(rpb[l]))
        x2d = _merge(x2d, ya.reshape(B * T, D_A), yb.reshape(B * T, D_B), m, w_pa[l].astype(bf16),
                     w_pb[l].astype(bf16), w_out[l].astype(bf16), row(final_g), l == depth - 1)
    return x2d.reshape(B, T, D_MODEL)


def kernel(x_prompt, x_sample, norm_g, w_in, shift_mu, w0, w2, a0, a2, k_k, k_a, r_k, gn_w, gn_b, rpb, w_pa,
           w_pb, w_out, final_g):
    params = (norm_g, w_in, shift_mu, w0, w2, a0, a2, k_k, k_a, r_k, gn_w, gn_b, rpb, w_pa, w_pb, w_out,
              final_g)
    return (_trunk(x_prompt, *params), _trunk(x_sample, *params))
```

```python
import functools
import math

import numpy as np
import jax
import jax.numpy as jnp
from jax import lax
from jax.experimental import pallas as pl
from jax.experimental.pallas import tpu as pltpu

f32 = jnp.float32
bf16 = jnp.bfloat16

D_MODEL = 1024
GRID_W = 64
D_A = 512
HEAD = 64
LORA = 64
GN_EPS = 6.4e-4
D_B = 512
MAX_KR = 8
WIN_C = 16
RMS_EPS = 1e-6
A_SHIFT = 3 * D_A + 4 * LORA
OFF_GA = A_SHIFT
OFF_QB = OFF_GA + D_A
OFF_GB = OFF_QB + 3 * D_B
OFF_MA = OFF_GB + D_B
D_IN = OFF_MA + 2 * D_MODEL

LANES = 128
N_PAIR = D_A // LANES
CHUNK = 64
FINISH_ROWS = 256
BF16_ROWS = 16
TOKEN_TILE = 512
VMEM_LIMIT = 56 * 1024 * 1024
LOG_DECAY_SCALE = -math.exp(-0.5)
Q_ROWS = 1
KEY_ROWS = MAX_KR

_NT = (((1,), (1,)), ((), ()))


def _sigmoid(x):
    return 1.0 / (1.0 + jnp.exp(-x))


def _dot(a, b):
    return jnp.dot(a, b, preferred_element_type=f32)


def _dot_nt(a, b):
    return lax.dot_general(a, b, _NT, preferred_element_type=f32)


def _rms_norm(x, g):
    return x * lax.rsqrt(jnp.mean(x * x, axis=-1, keepdims=True) + RMS_EPS) * g


_IN_SEGMENTS = ((0, OFF_GA), (OFF_GA, OFF_QB), (OFF_QB, OFF_GB), (OFF_GB, OFF_MA), (OFF_MA, D_IN))


def _inproj_kernel(x_ref, g_ref, w_ref, *out_refs):
    h = _rms_norm(x_ref[...], g_ref[...]).astype(bf16)
    for ref, (lo, hi) in zip(out_refs, _IN_SEGMENTS):
        ref[...] = _dot(h, w_ref[:, lo:hi]).astype(bf16)


def _inproj(x2d, g, w_bf16):
    n_tok = x2d.shape[0]
    tm = TOKEN_TILE
    const = lambda i: (0, 0)
    row = lambda i: (i, 0)
    return pl.pallas_call(
        _inproj_kernel,
        grid=(n_tok // tm,),
        in_specs=[
            pl.BlockSpec((tm, D_MODEL), row),
            pl.BlockSpec((1, D_MODEL), const),
            pl.BlockSpec((D_MODEL, D_IN), const, pipeline_mode=pl.Buffered(1)),
        ],
        out_specs=[pl.BlockSpec((tm, hi - lo), row) for lo, hi in _IN_SEGMENTS],
        out_shape=[jax.ShapeDtypeStruct((n_tok, hi - lo), bf16) for lo, hi in _IN_SEGMENTS],
        compiler_params=pltpu.CompilerParams(
            dimension_semantics=("arbitrary",), vmem_limit_bytes=VMEM_LIMIT),
        name="inproj",
    )(x2d, g, w_bf16)


(_OP_R, _OP_A, _OP_V_H0, _OP_V_H1, _OP_V) = range(5)
N_OPND = 5


def _rwkv_kernel(za_ref, ga_ref, mu_ref, w0_ref, w2_ref, a0_ref, a2_ref, kk_ref, ka_ref, rk_ref,
                 gnw_ref, gnb_ref, out_ref, h_scr, o_scr, bonus_scr, opnd_scr, yt_scr, bkt_scr, pcol_scr,
                 sol_t, sol_arb, sol_xv, *, seq_len):
    C = CHUNK
    n_chunks = seq_len // C
    window = C + 2 * BF16_ROWS
    lane = lax.broadcasted_iota(jnp.int32, (1, LANES), 1)
    head0 = lane < HEAD
    lane_all = lax.broadcasted_iota(jnp.int32, (1, D_A), 1)
    first_head = ((lane_all // HEAD) % 2 == 0)
    keep_h0 = first_head.astype(bf16)
    keep_h1 = jnp.logical_not(first_head).astype(bf16)
    t_i = lax.broadcasted_iota(jnp.int32, (C, 2 * C), 0)
    s_i = lax.broadcasted_iota(jnp.int32, (C, 2 * C), 1) % C
    strict = (t_i > s_i, t_i < s_i)
    incl = (t_i >= s_i, t_i <= s_i)
    eye_pair = (t_i == s_i).astype(f32)
    r2 = lax.broadcasted_iota(jnp.int32, (2 * C, 2 * C), 0)
    c2 = lax.broadcasted_iota(jnp.int32, (2 * C, 2 * C), 1)
    blockdiag = (r2 // C) == (c2 // C)
    seg_ones = blockdiag.astype(bf16)
    tt = lax.broadcasted_iota(jnp.int32, (C, C), 0)
    ss = lax.broadcasted_iota(jnp.int32, (C, C), 1)
    tri = ((tt >= ss).astype(bf16), (tt <= ss).astype(bf16))
    w_row = lax.broadcasted_iota(jnp.int32, (C, window), 0)
    w_col = lax.broadcasted_iota(jnp.int32, (C, window), 1)

    def stack(x):
        return jnp.concatenate([jnp.where(head0, x, 0.0), jnp.where(head0, 0.0, x)], axis=0).astype(bf16)

    def blockdiag_of(x):
        return jnp.where(blockdiag, jnp.concatenate([x, x], axis=0), 0.0).astype(bf16)

    def seg_sum(x):
        return jnp.concatenate(
            [_dot(x[:, p * LANES:(p + 1) * LANES].astype(bf16), seg_ones) for p in range(N_PAIR)], axis=1)

    def shifted_chunk(t0):
        start = pl.multiple_of(jnp.clip(t0 - BF16_ROWS, 0, seq_len - window), BF16_ROWS)
        off = t0 - start
        pick = jnp.logical_or(w_col == w_row + (off - 1), w_col == w_row + (off + 1))
        nb = _dot(pick.astype(bf16), za_ref[pl.ds(start, window), :])
        zc = za_ref[pl.ds(t0, C), :].astype(f32)
        return zc + mu_ref[...] * (0.5 * nb - zc)

    dirs = (0, 1)
    chains = [(d, p) for d in dirs for p in range(N_PAIR)]
    pair_lanes = [slice(p * LANES, (p + 1) * LANES) for p in range(N_PAIR)]

    def prepare(chunk, slot):
        t0 = [pl.multiple_of(chunk[d] * C, C) for d in dirs]
        z = [shifted_chunk(t0[d]) for d in dirs]
        r = [z[d][:, 0:D_A] for d in dirs]
        k = [z[d][:, D_A:2 * D_A] for d in dirs]
        v = [z[d][:, 2 * D_A:3 * D_A] for d in dirs]
        yield
        w_lo = [_dot(jnp.tanh(z[d][:, 3 * D_A:3 * D_A + 2 * LORA]).astype(bf16), w2_ref[d]) for d in dirs]
        a_lo = [_dot(z[d][:, 3 * D_A + 2 * LORA:A_SHIFT].astype(bf16), a2_ref[d]) for d in dirs]
        kk = [k[d] * kk_ref[...] for d in dirs]
        yield
        ld = [LOG_DECAY_SCALE * _sigmoid(w0_ref[d:d + 1, :] + w_lo[d]) for d in dirs]
        ld_hi = [ld[d].astype(bf16) for d in dirs]
        ld_lo = [(ld[d] - ld_hi[d].astype(f32)).astype(bf16) for d in dirs]
        cs = [_dot(tri[d], ld_hi[d]) + _dot(tri[d], ld_lo[d]) for d in dirs]
        yield
        a = [_sigmoid(a0_ref[d:d + 1, :] + a_lo[d]) for d in dirs]
        k_dir = [k[d] * (1.0 + (a[d] - 1.0) * ka_ref[...]) for d in dirs]
        sums = seg_sum(jnp.concatenate(
            [kk[d] * kk[d] for d in dirs] + [r[d] * k_dir[d] * rk_ref[...] for d in dirs], axis=0))
        kk = [kk[d] * lax.rsqrt(jnp.maximum(sums[d * C:(d + 1) * C], 1e-24)) for d in dirs]
        bonus_s = [sums[(2 + d) * C:(3 + d) * C] for d in dirs]
        yield
        e_pos = [jnp.exp(cs[d]) for d in dirs]
        e_neg = [jnp.exp(-cs[d]) for d in dirs]
        e_prev = [jnp.exp(cs[d] - ld[d]) for d in dirs]
        p_tot = [e_pos[0][C - 1:, :], e_pos[1][:1, :]]
        for d in dirs:
            a_t = (-kk[d] * e_prev[d]).astype(bf16)
            b_t = kk[d] * a[d] * e_neg[d]
            k_t = k_dir[d] * e_neg[d]
            b_hat = b_t * p_tot[d]
            k_hat = k_t * p_tot[d]
            v_b = v[d].astype(bf16)
            opnd_scr[slot, d, _OP_R] = (r[d] * e_pos[d]).astype(bf16)
            opnd_scr[slot, d, _OP_A] = a_t
            opnd_scr[slot, d, _OP_V_H0] = v_b * keep_h0
            opnd_scr[slot, d, _OP_V_H1] = v_b * keep_h1
            opnd_scr[slot, d, _OP_V] = v_b
            bonus_scr[d, pl.ds(t0[d], C), :] = bonus_s[d] * v[d]
            for p, sl in enumerate(pair_lanes):
                n = d * N_PAIR + p
                b_p, k_p = b_t[:, sl], k_t[:, sl]
                y_all = jnp.concatenate([jnp.where(head0, b_p, 0.0), jnp.where(head0, 0.0, b_p),
                                         jnp.where(head0, k_p, 0.0), jnp.where(head0, 0.0, k_p)], axis=0)
                yt_scr[slot, n] = y_all.T.astype(bf16)
                bkt_scr[slot, n] = jnp.concatenate([b_hat[:, sl], k_hat[:, sl]], axis=0).T.astype(bf16)
                pcol_scr[slot, n] = jnp.broadcast_to(p_tot[d][:, sl], (LANES, LANES)).T
        yield

    def per_chain(fn):
        return [fn(n, d, pair_lanes[p]) for n, (d, p) in enumerate(chains)]

    def rows_of(slot, d, sl, *ops):
        parts = [opnd_scr[slot, d, j, :, sl] for j in ops]
        return parts[0] if len(parts) == 1 else jnp.concatenate(parts, axis=0)

    def solve(slot):
        s = per_chain(lambda n, d, sl: _dot(rows_of(slot, d, sl, _OP_A, _OP_R), yt_scr[slot, n]))
        yield
        a_ab = per_chain(lambda n, d, sl: jnp.where(strict[d], s[n][:C, :2 * C], 0.0))
        a_xk = per_chain(lambda n, d, sl: jnp.concatenate(
            [jnp.where(strict[d], s[n][:C, 2 * C:], 0.0), jnp.where(incl[d], s[n][C:, 2 * C:], 0.0)],
            axis=0).astype(bf16))
        a_rb = per_chain(lambda n, d, sl: jnp.where(incl[d], s[n][C:, :2 * C], 0.0).astype(bf16))
        xv = per_chain(lambda n, d, sl: _dot(a_xk[n], rows_of(slot, d, sl, _OP_V_H0, _OP_V_H1)))
        a_pow = per_chain(lambda n, d, sl: _dot(a_ab[n].astype(bf16), blockdiag_of(a_ab[n])))
        t_inv = per_chain(lambda n, d, sl: eye_pair + a_ab[n])
        yield
        n_sq = 2
        while n_sq < C:
            last = 2 * n_sq >= C
            lhs = per_chain(lambda n, d, sl: (t_inv[n] if last else jnp.concatenate(
                [t_inv[n], a_pow[n]], axis=0)).astype(bf16))
            prod = per_chain(lambda n, d, sl: _dot(lhs[n], blockdiag_of(a_pow[n])))
            t_inv = per_chain(lambda n, d, sl: t_inv[n] + prod[n][:C])
            if not last:
                a_pow = per_chain(lambda n, d, sl: prod[n][C:])
            n_sq *= 2
            if last:
                for n in range(len(chains)):
                    sol_t[n] = t_inv[n].astype(bf16)
                    sol_arb[n] = a_rb[n]
                    sol_xv[n] = xv[n]
            yield

    def carry_state(slot, t0):
        h = [h_scr[n] for n in range(len(chains))]
        ah = per_chain(lambda n, d, sl: _dot(rows_of(slot, d, sl, _OP_A, _OP_R), h[n].astype(bf16)))
        yield
        u = per_chain(lambda n, d, sl: _dot(sol_t[n], stack(ah[n][:C] + sol_xv[n, :C, :])))
        yield
        o = per_chain(lambda n, d, sl: ah[n][C:] + _dot(sol_arb[n], stack(u[n])) + sol_xv[n, C:, :])
        upd = per_chain(lambda n, d, sl: _dot(
            bkt_scr[slot, n], jnp.concatenate([u[n].astype(bf16), rows_of(slot, d, sl, _OP_V)], axis=0)))
        for n in range(len(chains)):
            h_scr[n] = h[n] * pcol_scr[slot, n] + jnp.where(blockdiag, upd[n], 0.0)
        for d in dirs:
            o_scr[d, pl.ds(t0[d], C), :] = jnp.concatenate(o[d * N_PAIR:(d + 1) * N_PAIR], axis=1)
        yield

    def trip_chunks(j):
        return (jnp.minimum(j, n_chunks - 1), jnp.maximum(n_chunks - 1 - j, 0))

    def interleave(state, ahead, prep):
        order = "sapasapasapapap"
        stages = {"s": state, "a": ahead, "p": prep}
        for c in order:
            if stages[c] is not None:
                next(stages[c])

    def step(i, carry):
        cur = lax.rem(i, 2)
        chunk = trip_chunks(i)
        interleave(carry_state(cur, [pl.multiple_of(chunk[d] * C, C) for d in dirs]),
                   solve(1 - cur), prepare(trip_chunks(i + 2), cur))
        return carry

    h_scr[...] = jnp.zeros_like(h_scr)
    for _ in prepare(trip_chunks(0), 0):
        pass
    interleave(None, solve(0), prepare(trip_chunks(1), 1))
    lax.fori_loop(0, n_chunks, step, 0)

    def finish(i, carry):
        rows = pl.ds(pl.multiple_of(i * FINISH_ROWS, FINISH_ROWS), FINISH_ROWS)
        o = o_scr[0, rows, :] + o_scr[1, rows, :]
        mean = seg_sum(o) * (1.0 / HEAD)
        oc = o - mean
        var = seg_sum(oc * oc) * (1.0 / HEAD)
        y = oc * lax.rsqrt(var + GN_EPS) * gnw_ref[...] + gnb_ref[...]
        y = y + bonus_scr[0, rows, :] + bonus_scr[1, rows, :]
        gate = ga_ref[rows, :].astype(f32)
        out_ref[rows, :] = (y * (gate * _sigmoid(gate))).astype(bf16)
        return carry

    lax.fori_loop(0, seq_len // FINISH_ROWS, finish, 0)


def _rwkv(za, ga, mu, w0, w2p, a0, a2p, k_k, k_a, r_k, gn_w, gn_b):
    B, T, _ = za.shape
    seq = lambda width: pl.BlockSpec((None, T, width), lambda b: (b, 0, 0))
    full = lambda *shape: pl.BlockSpec(shape, lambda b: (0,) * len(shape))
    return pl.pallas_call(
        functools.partial(_rwkv_kernel, seq_len=T),
        grid=(B,),
        in_specs=[
            seq(A_SHIFT), seq(D_A), full(1, A_SHIFT), full(2, D_A), full(2, 2 * LORA, D_A),
            full(2, D_A), full(2, 2 * LORA, D_A), full(1, D_A), full(1, D_A), full(1, D_A),
            full(1, D_A), full(1, D_A),
        ],
        out_specs=seq(D_A),
        out_shape=jax.ShapeDtypeStruct((B, T, D_A), bf16),
        scratch_shapes=[
            pltpu.VMEM((2 * N_PAIR, LANES, LANES), f32),
            pltpu.VMEM((2, T, D_A), f32),
            pltpu.VMEM((2, T, D_A), f32),
            pltpu.VMEM((2, 2, N_OPND, CHUNK, D_A), bf16),
            pltpu.VMEM((2, 2 * N_PAIR, LANES, 4 * CHUNK), bf16),
            pltpu.VMEM((2, 2 * N_PAIR, LANES, 2 * CHUNK), bf16),
            pltpu.VMEM((2, 2 * N_PAIR, LANES, LANES), f32),
            pltpu.VMEM((2 * N_PAIR, CHUNK, 2 * CHUNK), bf16),
            pltpu.VMEM((2 * N_PAIR, CHUNK, 2 * CHUNK), bf16),
            pltpu.VMEM((2 * N_PAIR, 2 * CHUNK, LANES), f32),
        ],
        compiler_params=pltpu.CompilerParams(
            dimension_semantics=("arbitrary",), vmem_limit_bytes=VMEM_LIMIT),
        name="rwkv7",
    )(za, ga, mu, w0, w2p, a0, a2p, k_k, k_a, r_k, gn_w, gn_b)


def _natten_bias_table(rpb, n_rows):
    qc = np.arange(GRID_W)
    kc = np.arange(GRID_W)
    cs = np.clip(qc - WIN_C // 2, 0, GRID_W - WIN_C)
    col_mask = (kc[None, :] >= cs[:, None]) & (kc[None, :] < cs[:, None] + WIN_C)
    dc = np.clip(kc[None, :] - qc[:, None] + WIN_C - 1, 0, 2 * WIN_C - 2)
    tbl = jnp.where(col_mask[None, None], rpb.astype(f32)[:, :, dc], -1e30)
    n_heads = rpb.shape[0]
    first = _natten_window_offsets(n_rows)
    variants = []
    for a in range(KEY_ROWS):
        i = np.arange(KEY_ROWS)
        valid = (i >= first[a]) & (i < first[a] + MAX_KR)
        dr = np.clip(i - a + MAX_KR - 1, 0, 2 * MAX_KR - 2)
        t = jnp.where(valid[None, :, None, None], tbl[:, dr], -1e30)
        variants.append(t.transpose(0, 2, 1, 3).reshape(n_heads // 2, 2 * GRID_W, KEY_ROWS * GRID_W))
    return jnp.stack(variants, axis=0)


def _natten_step_rows(j, n_rows):
    r0 = Q_ROWS * j
    rs0 = jnp.clip(r0 - MAX_KR // 2, 0, n_rows - MAX_KR)
    return r0, jnp.minimum(rs0, n_rows - KEY_ROWS)


def _natten_window_offsets(n_rows):
    first = {}
    for j in range(n_rows // Q_ROWS):
        r0 = Q_ROWS * j
        st = min(int(np.clip(r0 - MAX_KR // 2, 0, n_rows - MAX_KR)), n_rows - KEY_ROWS)
        for r in range(r0, r0 + Q_ROWS):
            rs = int(np.clip(r - MAX_KR // 2, 0, n_rows - MAX_KR))
            assert 0 <= rs - st and rs - st + MAX_KR <= KEY_ROWS and 0 <= r - st < KEY_ROWS
            assert first.setdefault(r - st, rs - st) == rs - st
    return [first.get(a, 0) for a in range(KEY_ROWS)]


def _natten_kernel(qkv_ref, gb_ref, tbl_ref, out_ref, *, seq_len):
    n_rows = seq_len // GRID_W
    n_q = Q_ROWS * GRID_W
    n_keys = KEY_ROWS * GRID_W
    lane = lax.broadcasted_iota(jnp.int32, (1, LANES), 1)
    head0 = lane < HEAD

    def step(j, carry):
        r0, st = _natten_step_rows(j, n_rows)
        a0 = r0 - st
        q_rows = pl.ds(pl.multiple_of(r0 * GRID_W, n_q), n_q)
        k_rows = pl.ds(pl.multiple_of(st * GRID_W, GRID_W), n_keys)
        pairs = range(N_PAIR)
        lanes = [slice(p * LANES, (p + 1) * LANES) for p in pairs]

        def stacked(x):
            zero = jnp.zeros_like(x[:GRID_W])
            parts = []
            for t in range(Q_ROWS):
                xt = x[t * GRID_W:(t + 1) * GRID_W]
                parts += [jnp.where(head0, xt, zero), jnp.where(head0, zero, xt)]
            return jnp.concatenate(parts, axis=0)

        q2 = [stacked(qkv_ref[q_rows, lanes[p]] * (HEAD ** -0.5)) for p in pairs]
        bias = [jnp.concatenate([tbl_ref[a0 + t, p] for t in range(Q_ROWS)], axis=0) for p in pairs]
        s = [_dot_nt(q2[p], qkv_ref[k_rows, D_B + p * LANES:D_B + (p + 1) * LANES]) + bias[p] for p in pairs]
        e = [jnp.exp(x - jnp.max(x, axis=-1, keepdims=True)) for x in s]
        inv_den = [1.0 / jnp.sum(x, axis=-1, keepdims=True) for x in e]
        o = [_dot(e[p].astype(bf16), qkv_ref[k_rows, 2 * D_B + p * LANES:2 * D_B + (p + 1) * LANES]) * inv_den[p]
             for p in pairs]
        for p in pairs:
            gate = gb_ref[q_rows, lanes[p]].astype(f32)
            pair_out = jnp.concatenate(
                [jnp.where(head0, o[p][2 * t * GRID_W:(2 * t + 1) * GRID_W],
                           o[p][(2 * t + 1) * GRID_W:(2 * t + 2) * GRID_W]) for t in range(Q_ROWS)], axis=0)
            out_ref[q_rows, lanes[p]] = (pair_out * (gate * _sigmoid(gate))).astype(bf16)
        return carry

    lax.fori_loop(0, n_rows // Q_ROWS, step, 0)


def _natten(qkv, gb, tbl):
    B, T, _ = qkv.shape
    seq = lambda width: pl.BlockSpec((None, T, width), lambda b: (b, 0, 0))
    return pl.pallas_call(
        functools.partial(_natten_kernel, seq_len=T),
        grid=(B,),
        in_specs=[seq(3 * D_B), seq(D_B),
                  pl.BlockSpec(tbl.shape, lambda b: (0, 0, 0, 0), pipeline_mode=pl.Buffered(1))],
        out_specs=seq(D_B),
        out_shape=jax.ShapeDtypeStruct((B, T, D_B), bf16),
        compiler_params=pltpu.CompilerParams(
            dimension_semantics=("arbitrary",), vmem_limit_bytes=VMEM_LIMIT),
        name="natten",
    )(qkv, gb, tbl)


def _merge_kernel(x_ref, ya_ref, yb_ref, m_ref, wpa_ref, wpb_ref, wout_ref, fg_ref, out_ref, *, final_norm):
    m = m_ref[...].astype(f32)
    merged = (_sigmoid(m[:, :D_MODEL]) * _dot(ya_ref[...], wpa_ref[...])
              + _sigmoid(m[:, D_MODEL:]) * _dot(yb_ref[...], wpb_ref[...]))
    y = x_ref[...] + _dot(merged.astype(bf16), wout_ref[...])
    if final_norm:
        y = _rms_norm(y, fg_ref[...])
    out_ref[...] = y


def _merge(x2d, ya, yb, m, w_pa, w_pb, w_out, final_g, final_norm):
    n_tok = x2d.shape[0]
    tm = TOKEN_TILE
    const = lambda i: (0, 0)
    row = lambda i: (i, 0)
    return pl.pallas_call(
        functools.partial(_merge_kernel, final_norm=final_norm),
        grid=(n_tok // tm,),
        in_specs=[
            pl.BlockSpec((tm, D_MODEL), row), pl.BlockSpec((tm, D_A), row), pl.BlockSpec((tm, D_B), row),
            pl.BlockSpec((tm, 2 * D_MODEL), row),
            pl.BlockSpec((D_A, D_MODEL), const), pl.BlockSpec((D_B, D_MODEL), const),
            pl.BlockSpec((D_MODEL, D_MODEL), const), pl.BlockSpec((1, D_MODEL), const),
        ],
        out_specs=pl.BlockSpec((tm, D_MODEL), row),
        out_shape=jax.ShapeDtypeStruct((n_tok, D_MODEL), f32),
        compiler_params=pltpu.CompilerParams(
            dimension_semantics=("arbitrary",), vmem_limit_bytes=VMEM_LIMIT),
        name="merge",
    )(x2d, ya, yb, m, w_pa, w_pb, w_out, final_g)


def _pad_lora(w):
    z = jnp.zeros_like(w[0])
    return jnp.stack([jnp.concatenate([w[0], z], axis=0), jnp.concatenate([z, w[1]], axis=0)], axis=0).astype(bf16)


def _trunk(x, norm_g, w_in, shift_mu, w0, w2, a0, a2, k_k, k_a, r_k, gn_w, gn_b, rpb, w_pa, w_pb, w_out,
           final_g):
    B, T, _ = x.shape
    depth = norm_g.shape[0]
    x2d = x.reshape(B * T, D_MODEL)
    row = lambda a: a.reshape(1, -1).astype(f32)
    for l in range(depth):
        za, ga, qkv, gb, m = _inproj(x2d, row(norm_g[l]), w_in[l].astype(bf16))
        ya = _rwkv(za.reshape(B, T, A_SHIFT), ga.reshape(B, T, D_A), row(shift_mu[l]), w0[l].astype(f32),
                   _pad_lora(w2[l]), a0[l].astype(f32), _pad_lora(a2[l]), row(k_k[l]), row(k_a[l]),
                   row(r_k[l]), row(gn_w[l]), row(gn_b[l]))
        yb = _natten(qkv.reshape(B, T, 3 * D_B), gb.reshape(B, T, D_B), _natten_bias_table(rpb[l], T // GRID_W))
        x2d = _merge(x2d, ya.reshape(B * T, D_A), yb.reshape(B * T, D_B), m, w_pa[l].astype(bf16),
                     w_pb[l].astype(bf16), w_out[l].astype(bf16), row(final_g), l == depth - 1)
    return x2d.reshape(B, T, D_MODEL)


def kernel(x_prompt, x_sample, norm_g, w_in, shift_mu, w0, w2, a0, a2, k_k, k_a, r_k, gn_w, gn_b, rpb, w_pa,
           w_pb, w_out, final_g):
    params = (norm_g, w_in, shift_mu, w0, w2, a0, a2, k_k, k_a, r_k, gn_w, gn_b, rpb, w_pa, w_pb, w_out,
              final_g)
    return (_trunk(x_prompt, *params), _trunk(x_sample, *params))
```

```python
import functools
import math

import numpy as np
import jax
import jax.numpy as jnp
from jax import lax
from jax.experimental import pallas as pl
from jax.experimental.pallas import tpu as pltpu

f32 = jnp.float32
bf16 = jnp.bfloat16

D_MODEL = 1024
GRID_W = 64
D_A = 512
HEAD = 64
LORA = 64
GN_EPS = 6.4e-4
D_B = 512
MAX_KR = 8
WIN_C = 16
RMS_EPS = 1e-6
A_SHIFT = 3 * D_A + 4 * LORA
OFF_GA = A_SHIFT
OFF_QB = OFF_GA + D_A
OFF_GB = OFF_QB + 3 * D_B
OFF_MA = OFF_GB + D_B
D_IN = OFF_MA + 2 * D_MODEL

LANES = 128
N_PAIR = D_A // LANES
CHUNK = 64
FINISH_ROWS = 256
BF16_ROWS = 16
TOKEN_TILE = 512
VMEM_LIMIT = 56 * 1024 * 1024
LOG_DECAY_SCALE = -math.exp(-0.5)
Q_ROWS = 1
KEY_ROWS = MAX_KR

_NT = (((1,), (1,)), ((), ()))


def _sigmoid(x):
    return 1.0 / (1.0 + jnp.exp(-x))


def _dot(a, b):
    return jnp.dot(a, b, preferred_element_type=f32)


def _dot_nt(a, b):
    return lax.dot_general(a, b, _NT, preferred_element_type=f32)


def _rms_norm(x, g):
    return x * lax.rsqrt(jnp.mean(x * x, axis=-1, keepdims=True) + RMS_EPS) * g


_IN_SEGMENTS = ((0, OFF_GA), (OFF_GA, OFF_QB), (OFF_QB, OFF_GB), (OFF_GB, OFF_MA), (OFF_MA, D_IN))


def _inproj_kernel(x_ref, g_ref, w_ref, *out_refs):
    h = _rms_norm(x_ref[...], g_ref[...]).astype(bf16)
    for ref, (lo, hi) in zip(out_refs, _IN_SEGMENTS):
        ref[...] = _dot(h, w_ref[:, lo:hi]).astype(bf16)


def _layer_spec(array, layer, **kw):
    shape = array.shape[1:]
    return pl.BlockSpec((None,) + shape, lambda i: (layer,) + (0,) * len(shape), **kw)


def _inproj(x2d, g, w_bf16, layer):
    n_tok = x2d.shape[0]
    tm = TOKEN_TILE
    row = lambda i: (i, 0)
    return pl.pallas_call(
        _inproj_kernel,
        grid=(n_tok // tm,),
        in_specs=[
            pl.BlockSpec((tm, D_MODEL), row),
            _layer_spec(g, layer),
            _layer_spec(w_bf16, layer, pipeline_mode=pl.Buffered(1)),
        ],
        out_specs=[pl.BlockSpec((tm, hi - lo), row) for lo, hi in _IN_SEGMENTS],
        out_shape=[jax.ShapeDtypeStruct((n_tok, hi - lo), bf16) for lo, hi in _IN_SEGMENTS],
        compiler_params=pltpu.CompilerParams(
            dimension_semantics=("arbitrary",), vmem_limit_bytes=VMEM_LIMIT),
        name="inproj",
    )(x2d, g, w_bf16)


(_OP_R, _OP_A, _OP_V_H0, _OP_V_H1, _OP_V) = range(5)
N_OPND = 5


def _rwkv_kernel(za_ref, ga_ref, mu_ref, w0_ref, w2_ref, a0_ref, a2_ref, kk_ref, ka_ref, rk_ref,
                 gnw_ref, gnb_ref, out_ref, h_scr, o_scr, bonus_scr, opnd_scr, yt_scr, bkt_scr, pcol_scr,
                 sol_t, sol_arb, sol_xv, *, seq_len):
    C = CHUNK
    n_chunks = seq_len // C
    window = C + 2 * BF16_ROWS
    lane = lax.broadcasted_iota(jnp.int32, (1, LANES), 1)
    head0 = lane < HEAD
    lane_all = lax.broadcasted_iota(jnp.int32, (1, D_A), 1)
    first_head = ((lane_all // HEAD) % 2 == 0)
    keep_h0 = first_head.astype(bf16)
    keep_h1 = jnp.logical_not(first_head).astype(bf16)
    t_i = lax.broadcasted_iota(jnp.int32, (C, 2 * C), 0)
    s_i = lax.broadcasted_iota(jnp.int32, (C, 2 * C), 1) % C
    strict = (t_i > s_i, t_i < s_i)
    incl = (t_i >= s_i, t_i <= s_i)
    eye_pair = (t_i == s_i).astype(f32)
    r2 = lax.broadcasted_iota(jnp.int32, (2 * C, 2 * C), 0)
    c2 = lax.broadcasted_iota(jnp.int32, (2 * C, 2 * C), 1)
    blockdiag = (r2 // C) == (c2 // C)
    seg_ones = blockdiag.astype(bf16)
    tt = lax.broadcasted_iota(jnp.int32, (C, C), 0)
    ss = lax.broadcasted_iota(jnp.int32, (C, C), 1)
    tri = ((tt >= ss).astype(bf16), (tt <= ss).astype(bf16))
    w_row = lax.broadcasted_iota(jnp.int32, (C, window), 0)
    w_col = lax.broadcasted_iota(jnp.int32, (C, window), 1)

    def stack(x):
        return jnp.concatenate([jnp.where(head0, x, 0.0), jnp.where(head0, 0.0, x)], axis=0).astype(bf16)

    def blockdiag_of(x):
        return jnp.where(blockdiag, jnp.concatenate([x, x], axis=0), 0.0).astype(bf16)

    def seg_sum(x):
        return jnp.concatenate(
            [_dot(x[:, p * LANES:(p + 1) * LANES].astype(bf16), seg_ones) for p in range(N_PAIR)], axis=1)

    def shifted_chunk(t0):
        start = pl.multiple_of(jnp.clip(t0 - BF16_ROWS, 0, seq_len - window), BF16_ROWS)
        off = t0 - start
        pick = jnp.logical_or(w_col == w_row + (off - 1), w_col == w_row + (off + 1))
        nb = _dot(pick.astype(bf16), za_ref[pl.ds(start, window), :])
        zc = za_ref[pl.ds(t0, C), :].astype(f32)
        return zc + mu_ref[...] * (0.5 * nb - zc)

    dirs = (0, 1)
    chains = [(d, p) for d in dirs for p in range(N_PAIR)]
    pair_lanes = [slice(p * LANES, (p + 1) * LANES) for p in range(N_PAIR)]

    def prepare(chunk, slot):
        t0 = [pl.multiple_of(chunk[d] * C, C) for d in dirs]
        z = [shifted_chunk(t0[d]) for d in dirs]
        r = [z[d][:, 0:D_A] for d in dirs]
        k = [z[d][:, D_A:2 * D_A] for d in dirs]
        v = [z[d][:, 2 * D_A:3 * D_A] for d in dirs]
        yield
        w_lo = [_dot(jnp.tanh(z[d][:, 3 * D_A:3 * D_A + 2 * LORA]).astype(bf16), w2_ref[d]) for d in dirs]
        a_lo = [_dot(z[d][:, 3 * D_A + 2 * LORA:A_SHIFT].astype(bf16), a2_ref[d]) for d in dirs]
        kk = [k[d] * kk_ref[...] for d in dirs]
        yield
        ld = [LOG_DECAY_SCALE * _sigmoid(w0_ref[d:d + 1, :] + w_lo[d]) for d in dirs]
        ld_hi = [ld[d].astype(bf16) for d in dirs]
        ld_lo = [(ld[d] - ld_hi[d].astype(f32)).astype(bf16) for d in dirs]
        cs = [_dot(tri[d], ld_hi[d]) + _dot(tri[d], ld_lo[d]) for d in dirs]
        yield
        a = [_sigmoid(a0_ref[d:d + 1, :] + a_lo[d]) for d in dirs]
        k_dir = [k[d] * (1.0 + (a[d] - 1.0) * ka_ref[...]) for d in dirs]
        sums = seg_sum(jnp.concatenate(
            [kk[d] * kk[d] for d in dirs] + [r[d] * k_dir[d] * rk_ref[...] for d in dirs], axis=0))
        kk = [kk[d] * lax.rsqrt(jnp.maximum(sums[d * C:(d + 1) * C], 1e-24)) for d in dirs]
        bonus_s = [sums[(2 + d) * C:(3 + d) * C] for d in dirs]
        yield
        e_pos = [jnp.exp(cs[d]) for d in dirs]
        e_neg = [jnp.exp(-cs[d]) for d in dirs]
        e_prev = [jnp.exp(cs[d] - ld[d]) for d in dirs]
        p_tot = [e_pos[0][C - 1:, :], e_pos[1][:1, :]]
        for d in dirs:
            a_t = (-kk[d] * e_prev[d]).astype(bf16)
            b_t = kk[d] * a[d] * e_neg[d]
            k_t = k_dir[d] * e_neg[d]
            b_hat = b_t * p_tot[d]
            k_hat = k_t * p_tot[d]
            v_b = v[d].astype(bf16)
            opnd_scr[slot, d, _OP_R] = (r[d] * e_pos[d]).astype(bf16)
            opnd_scr[slot, d, _OP_A] = a_t
            opnd_scr[slot, d, _OP_V_H0] = v_b * keep_h0
            opnd_scr[slot, d, _OP_V_H1] = v_b * keep_h1
            opnd_scr[slot, d, _OP_V] = v_b
            bonus_scr[d, pl.ds(t0[d], C), :] = bonus_s[d] * v[d]
            for p, sl in enumerate(pair_lanes):
                n = d * N_PAIR + p
                b_p, k_p = b_t[:, sl], k_t[:, sl]
                y_all = jnp.concatenate([jnp.where(head0, b_p, 0.0), jnp.where(head0, 0.0, b_p),
                                         jnp.where(head0, k_p, 0.0), jnp.where(head0, 0.0, k_p)], axis=0)
                yt_scr[slot, n] = y_all.T.astype(bf16)
                bkt_scr[slot, n] = jnp.concatenate([b_hat[:, sl], k_hat[:, sl]], axis=0).T.astype(bf16)
                pcol_scr[slot, n] = jnp.broadcast_to(p_tot[d][:, sl], (LANES, LANES)).T
        yield

    def per_chain(fn):
        return [fn(n, d, pair_lanes[p]) for n, (d, p) in enumerate(chains)]

    def rows_of(slot, d, sl, *ops):
        parts = [opnd_scr[slot, d, j, :, sl] for j in ops]
        return parts[0] if len(parts) == 1 else jnp.concatenate(parts, axis=0)

    def solve(slot):
        s = per_chain(lambda n, d, sl: _dot(rows_of(slot, d, sl, _OP_A, _OP_R), yt_scr[slot, n]))
        yield
        a_ab = per_chain(lambda n, d, sl: jnp.where(strict[d], s[n][:C, :2 * C], 0.0))
        a_xk = per_chain(lambda n, d, sl: jnp.concatenate(
            [jnp.where(strict[d], s[n][:C, 2 * C:], 0.0), jnp.where(incl[d], s[n][C:, 2 * C:], 0.0)],
            axis=0).astype(bf16))
        a_rb = per_chain(lambda n, d, sl: jnp.where(incl[d], s[n][C:, :2 * C], 0.0).astype(bf16))
        xv = per_chain(lambda n, d, sl: _dot(a_xk[n], rows_of(slot, d, sl, _OP_V_H0, _OP_V_H1)))
        a_pow = per_chain(lambda n, d, sl: _dot(a_ab[n].astype(bf16), blockdiag_of(a_ab[n])))
        t_inv = per_chain(lambda n, d, sl: eye_pair + a_ab[n])
        yield
        n_sq = 2
        while n_sq < C:
            last = 2 * n_sq >= C
            lhs = per_chain(lambda n, d, sl: (t_inv[n] if last else jnp.concatenate(
                [t_inv[n], a_pow[n]], axis=0)).astype(bf16))
            prod = per_chain(lambda n, d, sl: _dot(lhs[n], blockdiag_of(a_pow[n])))
            t_inv = per_chain(lambda n, d, sl: t_inv[n] + prod[n][:C])
            if not last:
                a_pow = per_chain(lambda n, d, sl: prod[n][C:])
            n_sq *= 2
            if last:
                for n in range(len(chains)):
                    sol_t[n] = t_inv[n].astype(bf16)
                    sol_arb[n] = a_rb[n]
                    sol_xv[n] = xv[n]
            yield

    def carry_state(slot, t0):
        h = [h_scr[n] for n in range(len(chains))]
        ah = per_chain(lambda n, d, sl: _dot(rows_of(slot, d, sl, _OP_A, _OP_R), h[n].astype(bf16)))
        yield
        u = per_chain(lambda n, d, sl: _dot(sol_t[n], stack(ah[n][:C] + sol_xv[n, :C, :])))
        yield
        o = per_chain(lambda n, d, sl: ah[n][C:] + _dot(sol_arb[n], stack(u[n])) + sol_xv[n, C:, :])
        upd = per_chain(lambda n, d, sl: _dot(
            bkt_scr[slot, n], jnp.concatenate([u[n].astype(bf16), rows_of(slot, d, sl, _OP_V)], axis=0)))
        for n in range(len(chains)):
            h_scr[n] = h[n] * pcol_scr[slot, n] + jnp.where(blockdiag, upd[n], 0.0)
        for d in dirs:
            o_scr[d, pl.ds(t0[d], C), :] = jnp.concatenate(o[d * N_PAIR:(d + 1) * N_PAIR], axis=1)
        yield

    def trip_chunks(j):
        return (jnp.minimum(j, n_chunks - 1), jnp.maximum(n_chunks - 1 - j, 0))

    def finish(done_trip):
        both = [pl.ds(pl.multiple_of(c * C, C), C) for c in (done_trip, n_chunks - 1 - done_trip)]
        o = jnp.concatenate([o_scr[0, rows, :] + o_scr[1, rows, :] for rows in both], axis=0)
        mean = seg_sum(o) * (1.0 / HEAD)
        yield
        oc = o - mean
        var = seg_sum(oc * oc) * (1.0 / HEAD)
        yield
        y = oc * lax.rsqrt(var + GN_EPS) * gnw_ref[...] + gnb_ref[...]
        for j, rows in enumerate(both):
            gate = ga_ref[rows, :].astype(f32)
            y_j = y[j * C:(j + 1) * C] + bonus_scr[0, rows, :] + bonus_scr[1, rows, :]
            out_ref[rows, :] = (y_j * (gate * _sigmoid(gate))).astype(bf16)
        yield

    def interleave(state=None, ahead=None, prep=None, fin=None):
        stages = {"s": state, "a": ahead, "p": prep, "f": fin}
        for c in "sapfasapfasapfapap":
            if stages[c] is not None:
                next(stages[c])

    def trip(i, with_solve=True, with_prep=True, with_finish=False):
        cur = i % 2
        chunk = trip_chunks(i)
        interleave(state=carry_state(cur, [pl.multiple_of(chunk[d] * C, C) for d in dirs]),
                   ahead=solve(1 - cur) if with_solve else None,
                   prep=prepare(trip_chunks(i + 2), cur) if with_prep else None,
                   fin=finish(i - 1) if with_finish else None)

    def loop(lo, hi, body=trip, **kw):
        def step(i, carry):
            body(i, **kw)
            return carry
        lax.fori_loop(lo, hi, step, 0)

    first_done = n_chunks // 2
    h_scr[...] = jnp.zeros_like(h_scr)
    interleave(prep=prepare(trip_chunks(0), 0))
    interleave(ahead=solve(0), prep=prepare(trip_chunks(1), 1))
    loop(0, first_done + 1)
    loop(first_done + 1, n_chunks - 2, with_finish=True)
    loop(n_chunks - 2, n_chunks - 1, with_prep=False, with_finish=True)
    loop(n_chunks - 1, n_chunks, with_solve=False, with_prep=False, with_finish=True)
    loop(n_chunks - 1, n_chunks, body=lambda i: interleave(fin=finish(i)))


def _rwkv(za, ga, layer_params, layer):
    B, T, _ = za.shape
    seq = lambda width: pl.BlockSpec((None, T, width), lambda b: (b, 0, 0))
    return pl.pallas_call(
        functools.partial(_rwkv_kernel, seq_len=T),
        grid=(B,),
        in_specs=[seq(A_SHIFT), seq(D_A)] + [_layer_spec(p, layer) for p in layer_params],
        out_specs=seq(D_A),
        out_shape=jax.ShapeDtypeStruct((B, T, D_A), bf16),
        scratch_shapes=[
            pltpu.VMEM((2 * N_PAIR, LANES, LANES), f32),
            pltpu.VMEM((2, T, D_A), f32),
            pltpu.VMEM((2, T, D_A), f32),
            pltpu.VMEM((2, 2, N_OPND, CHUNK, D_A), bf16),
            pltpu.VMEM((2, 2 * N_PAIR, LANES, 4 * CHUNK), bf16),
            pltpu.VMEM((2, 2 * N_PAIR, LANES, 2 * CHUNK), bf16),
            pltpu.VMEM((2, 2 * N_PAIR, LANES, LANES), f32),
            pltpu.VMEM((2 * N_PAIR, CHUNK, 2 * CHUNK), bf16),
            pltpu.VMEM((2 * N_PAIR, CHUNK, 2 * CHUNK), bf16),
            pltpu.VMEM((2 * N_PAIR, 2 * CHUNK, LANES), f32),
        ],
        compiler_params=pltpu.CompilerParams(
            dimension_semantics=("arbitrary",), vmem_limit_bytes=VMEM_LIMIT),
        name="rwkv7",
    )(za, ga, *layer_params)


def _natten_bias_table(rpb, n_rows):
    qc = np.arange(GRID_W)
    kc = np.arange(GRID_W)
    cs = np.clip(qc - WIN_C // 2, 0, GRID_W - WIN_C)
    col_mask = (kc[None, :] >= cs[:, None]) & (kc[None, :] < cs[:, None] + WIN_C)
    dc = np.clip(kc[None, :] - qc[:, None] + WIN_C - 1, 0, 2 * WIN_C - 2)
    tbl = jnp.where(col_mask[None, None], rpb.astype(f32)[:, :, dc], -1e30)
    n_heads = rpb.shape[0]
    first = _natten_window_offsets(n_rows)
    variants = []
    for a in range(KEY_ROWS):
        i = np.arange(KEY_ROWS)
        valid = (i >= first[a]) & (i < first[a] + MAX_KR)
        dr = np.clip(i - a + MAX_KR - 1, 0, 2 * MAX_KR - 2)
        t = jnp.where(valid[None, :, None, None], tbl[:, dr], -1e30)
        variants.append(t.transpose(0, 2, 1, 3).reshape(n_heads // 2, 2 * GRID_W, KEY_ROWS * GRID_W))
    return jnp.stack(variants, axis=0)


def _natten_step_rows(j, n_rows):
    r0 = Q_ROWS * j
    rs0 = jnp.clip(r0 - MAX_KR // 2, 0, n_rows - MAX_KR)
    return r0, jnp.minimum(rs0, n_rows - KEY_ROWS)


def _natten_window_offsets(n_rows):
    first = {}
    for j in range(n_rows // Q_ROWS):
        r0 = Q_ROWS * j
        st = min(int(np.clip(r0 - MAX_KR // 2, 0, n_rows - MAX_KR)), n_rows - KEY_ROWS)
        for r in range(r0, r0 + Q_ROWS):
            rs = int(np.clip(r - MAX_KR // 2, 0, n_rows - MAX_KR))
            assert 0 <= rs - st and rs - st + MAX_KR <= KEY_ROWS and 0 <= r - st < KEY_ROWS
            assert first.setdefault(r - st, rs - st) == rs - st
    return [first.get(a, 0) for a in range(KEY_ROWS)]


def _natten_kernel(qkv_ref, gb_ref, tbl_ref, out_ref, *, seq_len):
    n_rows = seq_len // GRID_W
    n_q = Q_ROWS * GRID_W
    n_keys = KEY_ROWS * GRID_W
    lane = lax.broadcasted_iota(jnp.int32, (1, LANES), 1)
    head0 = lane < HEAD

    def step(j, carry):
        r0, st = _natten_step_rows(j, n_rows)
        a0 = r0 - st
        q_rows = pl.ds(pl.multiple_of(r0 * GRID_W, n_q), n_q)
        k_rows = pl.ds(pl.multiple_of(st * GRID_W, GRID_W), n_keys)
        pairs = range(N_PAIR)
        lanes = [slice(p * LANES, (p + 1) * LANES) for p in pairs]

        def stacked(x):
            zero = jnp.zeros_like(x[:GRID_W])
            parts = []
            for t in range(Q_ROWS):
                xt = x[t * GRID_W:(t + 1) * GRID_W]
                parts += [jnp.where(head0, xt, zero), jnp.where(head0, zero, xt)]
            return jnp.concatenate(parts, axis=0)

        q2 = [stacked(qkv_ref[q_rows, lanes[p]] * (HEAD ** -0.5)) for p in pairs]
        bias = [jnp.concatenate([tbl_ref[a0 + t, p] for t in range(Q_ROWS)], axis=0) for p in pairs]
        s = [_dot_nt(q2[p], qkv_ref[k_rows, D_B + p * LANES:D_B + (p + 1) * LANES]) + bias[p] for p in pairs]
        e = [jnp.exp(x - jnp.max(x, axis=-1, keepdims=True)) for x in s]
        inv_den = [1.0 / jnp.sum(x, axis=-1, keepdims=True) for x in e]
        o = [_dot(e[p].astype(bf16), qkv_ref[k_rows, 2 * D_B + p * LANES:2 * D_B + (p + 1) * LANES]) * inv_den[p]
             for p in pairs]
        for p in pairs:
            gate = gb_ref[q_rows, lanes[p]].astype(f32)
            pair_out = jnp.concatenate(
                [jnp.where(head0, o[p][2 * t * GRID_W:(2 * t + 1) * GRID_W],
                           o[p][(2 * t + 1) * GRID_W:(2 * t + 2) * GRID_W]) for t in range(Q_ROWS)], axis=0)
            out_ref[q_rows, lanes[p]] = (pair_out * (gate * _sigmoid(gate))).astype(bf16)
        return carry

    lax.fori_loop(0, n_rows // Q_ROWS, step, 0)


def _natten(qkv, gb, tbl, layer):
    B, T, _ = qkv.shape
    seq = lambda width: pl.BlockSpec((None, T, width), lambda b: (b, 0, 0))
    return pl.pallas_call(
        functools.partial(_natten_kernel, seq_len=T),
        grid=(B,),
        in_specs=[seq(3 * D_B), seq(D_B), _layer_spec(tbl, layer, pipeline_mode=pl.Buffered(1))],
        out_specs=seq(D_B),
        out_shape=jax.ShapeDtypeStruct((B, T, D_B), bf16),
        compiler_params=pltpu.CompilerParams(
            dimension_semantics=("arbitrary",), vmem_limit_bytes=VMEM_LIMIT),
        name="natten",
    )(qkv, gb, tbl)


def _merge_kernel(x_ref, ya_ref, yb_ref, m_ref, wpa_ref, wpb_ref, wout_ref, fg_ref, out_ref, *, final_norm):
    m = m_ref[...].astype(f32)
    merged = (_sigmoid(m[:, :D_MODEL]) * _dot(ya_ref[...], wpa_ref[...])
              + _sigmoid(m[:, D_MODEL:]) * _dot(yb_ref[...], wpb_ref[...]))
    y = x_ref[...] + _dot(merged.astype(bf16), wout_ref[...])
    if final_norm:
        y = _rms_norm(y, fg_ref[...])
    out_ref[...] = y


def _merge(x2d, ya, yb, m, w_pa, w_pb, w_out, final_g, layer, final_norm):
    n_tok = x2d.shape[0]
    tm = TOKEN_TILE
    const = lambda i: (0, 0)
    row = lambda i: (i, 0)
    return pl.pallas_call(
        functools.partial(_merge_kernel, final_norm=final_norm),
        grid=(n_tok // tm,),
        in_specs=[
            pl.BlockSpec((tm, D_MODEL), row), pl.BlockSpec((tm, D_A), row), pl.BlockSpec((tm, D_B), row),
            pl.BlockSpec((tm, 2 * D_MODEL), row),
            _layer_spec(w_pa, layer), _layer_spec(w_pb, layer), _layer_spec(w_out, layer),
            pl.BlockSpec((1, D_MODEL), const),
        ],
        out_specs=pl.BlockSpec((tm, D_MODEL), row),
        out_shape=jax.ShapeDtypeStruct((n_tok, D_MODEL), f32),
        compiler_params=pltpu.CompilerParams(
            dimension_semantics=("arbitrary",), vmem_limit_bytes=VMEM_LIMIT),
        name="merge",
    )(x2d, ya, yb, m, w_pa, w_pb, w_out, final_g)


def _pad_lora(w):
    z = jnp.zeros_like(w[:, 0])
    return jnp.stack([jnp.concatenate([w[:, 0], z], axis=1), jnp.concatenate([z, w[:, 1]], axis=1)],
                     axis=1).astype(bf16)


def _prepare_params(norm_g, w_in, shift_mu, w0, w2, a0, a2, k_k, k_a, r_k, gn_w, gn_b, rpb, w_pa, w_pb, w_out,
                    final_g, n_rows):
    depth = norm_g.shape[0]
    rows = lambda a: a.reshape(depth, 1, -1).astype(f32)
    return dict(
        norm_g=rows(norm_g), w_in=w_in.astype(bf16),
        rwkv=(rows(shift_mu), w0.astype(f32), _pad_lora(w2), a0.astype(f32), _pad_lora(a2), rows(k_k),
              rows(k_a), rows(r_k), rows(gn_w), rows(gn_b)),
        tbl=jax.vmap(lambda r: _natten_bias_table(r, n_rows))(rpb),
        w_pa=w_pa.astype(bf16), w_pb=w_pb.astype(bf16), w_out=w_out.astype(bf16),
        final_g=final_g.reshape(1, -1).astype(f32), depth=depth)


def _trunk(x, p):
    B, T, _ = x.shape
    x2d = x.reshape(B * T, D_MODEL)
    for l in range(p["depth"]):
        za, ga, qkv, gb, m = _inproj(x2d, p["norm_g"], p["w_in"], l)
        ya = _rwkv(za.reshape(B, T, A_SHIFT), ga.reshape(B, T, D_A), p["rwkv"], l)
        yb = _natten(qkv.reshape(B, T, 3 * D_B), gb.reshape(B, T, D_B), p["tbl"], l)
        x2d = _merge(x2d, ya.reshape(B * T, D_A), yb.reshape(B * T, D_B), m, p["w_pa"], p["w_pb"], p["w_out"],
                     p["final_g"], l, l == p["depth"] - 1)
    return x2d.reshape(B, T, D_MODEL)


def kernel(x_prompt, x_sample, norm_g, w_in, shift_mu, w0, w2, a0, a2, k_k, k_a, r_k, gn_w, gn_b, rpb, w_pa,
           w_pb, w_out, final_g):
    assert x_prompt.shape[1] == x_sample.shape[1]
    p = _prepare_params(norm_g, w_in, shift_mu, w0, w2, a0, a2, k_k, k_a, r_k, gn_w, gn_b, rpb, w_pa, w_pb,
                        w_out, final_g, x_prompt.shape[1] // GRID_W)
    return (_trunk(x_prompt, p), _trunk(x_sample, p))
```

```python
import functools
import math

import numpy as np
import jax
import jax.numpy as jnp
from jax import lax
from jax.experimental import pallas as pl
from jax.experimental.pallas import tpu as pltpu

f32 = jnp.float32
bf16 = jnp.bfloat16

D_MODEL = 1024
GRID_W = 64
D_A = 512
HEAD = 64
LORA = 64
GN_EPS = 6.4e-4
D_B = 512
MAX_KR = 8
WIN_C = 16
RMS_EPS = 1e-6
A_SHIFT = 3 * D_A + 4 * LORA
OFF_GA = A_SHIFT
OFF_QB = OFF_GA + D_A
OFF_GB = OFF_QB + 3 * D_B
OFF_MA = OFF_GB + D_B
D_IN = OFF_MA + 2 * D_MODEL

LANES = 128
N_PAIR = D_A // LANES
CHUNK = 64
FINISH_ROWS = 256
BF16_ROWS = 16
TOKEN_TILE = 512
VMEM_LIMIT = 56 * 1024 * 1024
LOG_DECAY_SCALE = -math.exp(-0.5)
Q_ROWS = 1
KEY_ROWS = MAX_KR

_NT = (((1,), (1,)), ((), ()))


def _sigmoid(x):
    return 1.0 / (1.0 + jnp.exp(-x))


def _dot(a, b):
    return jnp.dot(a, b, preferred_element_type=f32)


def _dot_nt(a, b):
    return lax.dot_general(a, b, _NT, preferred_element_type=f32)


def _rms_norm(x, g):
    return x * lax.rsqrt(jnp.mean(x * x, axis=-1, keepdims=True) + RMS_EPS) * g


_IN_SEGMENTS = ((0, OFF_GA), (OFF_GA, OFF_QB), (OFF_QB, OFF_GB), (OFF_GB, OFF_MA), (OFF_MA, D_IN))


def _inproj_kernel(x_ref, g_ref, w_ref, *out_refs):
    h = _rms_norm(x_ref[...], g_ref[...]).astype(bf16)
    for ref, (lo, hi) in zip(out_refs, _IN_SEGMENTS):
        ref[...] = _dot(h, w_ref[:, lo:hi]).astype(bf16)


def _layer_spec(array, layer, **kw):
    shape = array.shape[1:]
    return pl.BlockSpec((None,) + shape, lambda i: (layer,) + (0,) * len(shape), **kw)


def _inproj(x2d, g, w_bf16, layer):
    n_tok = x2d.shape[0]
    tm = TOKEN_TILE
    row = lambda i: (i, 0)
    return pl.pallas_call(
        _inproj_kernel,
        grid=(n_tok // tm,),
        in_specs=[
            pl.BlockSpec((tm, D_MODEL), row),
            _layer_spec(g, layer),
            _layer_spec(w_bf16, layer, pipeline_mode=pl.Buffered(1)),
        ],
        out_specs=[pl.BlockSpec((tm, hi - lo), row) for lo, hi in _IN_SEGMENTS],
        out_shape=[jax.ShapeDtypeStruct((n_tok, hi - lo), bf16) for lo, hi in _IN_SEGMENTS],
        compiler_params=pltpu.CompilerParams(
            dimension_semantics=("arbitrary",), vmem_limit_bytes=VMEM_LIMIT),
        name="inproj",
    )(x2d, g, w_bf16)


(_OP_R, _OP_A, _OP_V_H0, _OP_V_H1, _OP_V) = range(5)
N_OPND = 5


def _rwkv_kernel(za_ref, ga_ref, mu_ref, w0_ref, w2_ref, a0_ref, a2_ref, kk_ref, ka_ref, rk_ref,
                 gnw_ref, gnb_ref, out_ref, h_scr, o_scr, bonus_scr, opnd_scr, yt_scr, bkt_scr, pcol_scr,
                 sol_t, sol_arb, sol_xv, *, seq_len):
    C = CHUNK
    n_chunks = seq_len // C
    window = C + 2 * BF16_ROWS
    lane = lax.broadcasted_iota(jnp.int32, (1, LANES), 1)
    head0 = lane < HEAD
    lane_all = lax.broadcasted_iota(jnp.int32, (1, D_A), 1)
    first_head = ((lane_all // HEAD) % 2 == 0)
    keep_h0 = first_head.astype(bf16)
    keep_h1 = jnp.logical_not(first_head).astype(bf16)
    t_i = lax.broadcasted_iota(jnp.int32, (C, 2 * C), 0)
    s_i = lax.broadcasted_iota(jnp.int32, (C, 2 * C), 1) % C
    strict = (t_i > s_i, t_i < s_i)
    incl = (t_i >= s_i, t_i <= s_i)
    eye_pair = (t_i == s_i).astype(f32)
    r2 = lax.broadcasted_iota(jnp.int32, (2 * C, 2 * C), 0)
    c2 = lax.broadcasted_iota(jnp.int32, (2 * C, 2 * C), 1)
    blockdiag = (r2 // C) == (c2 // C)
    seg_ones = blockdiag.astype(bf16)
    tt = lax.broadcasted_iota(jnp.int32, (C, C), 0)
    ss = lax.broadcasted_iota(jnp.int32, (C, C), 1)
    tri = ((tt >= ss).astype(bf16), (tt <= ss).astype(bf16))
    w_row = lax.broadcasted_iota(jnp.int32, (C, window), 0)
    w_col = lax.broadcasted_iota(jnp.int32, (C, window), 1)

    def stack(x):
        return jnp.concatenate([jnp.where(head0, x, 0.0), jnp.where(head0, 0.0, x)], axis=0).astype(bf16)

    def blockdiag_of(x):
        return jnp.where(blockdiag, jnp.concatenate([x, x], axis=0), 0.0).astype(bf16)

    def seg_sum(x):
        return jnp.concatenate(
            [_dot(x[:, p * LANES:(p + 1) * LANES].astype(bf16), seg_ones) for p in range(N_PAIR)], axis=1)

    def shifted_chunk(t0):
        start = pl.multiple_of(jnp.clip(t0 - BF16_ROWS, 0, seq_len - window), BF16_ROWS)
        off = t0 - start
        pick = jnp.logical_or(w_col == w_row + (off - 1), w_col == w_row + (off + 1))
        nb = _dot(pick.astype(bf16), za_ref[pl.ds(start, window), :])
        zc = za_ref[pl.ds(t0, C), :].astype(f32)
        return zc + mu_ref[...] * (0.5 * nb - zc)

    dirs = (0, 1)
    chains = [(d, p) for d in dirs for p in range(N_PAIR)]
    pair_lanes = [slice(p * LANES, (p + 1) * LANES) for p in range(N_PAIR)]

    def prepare(chunk, slot):
        t0 = [pl.multiple_of(chunk[d] * C, C) for d in dirs]
        z = [shifted_chunk(t0[d]) for d in dirs]
        r = [z[d][:, 0:D_A] for d in dirs]
        k = [z[d][:, D_A:2 * D_A] for d in dirs]
        v = [z[d][:, 2 * D_A:3 * D_A] for d in dirs]
        yield
        w_lo = [_dot(jnp.tanh(z[d][:, 3 * D_A:3 * D_A + 2 * LORA]).astype(bf16), w2_ref[d]) for d in dirs]
        a_lo = [_dot(z[d][:, 3 * D_A + 2 * LORA:A_SHIFT].astype(bf16), a2_ref[d]) for d in dirs]
        kk = [k[d] * kk_ref[...] for d in dirs]
        yield
        ld = [LOG_DECAY_SCALE * _sigmoid(w0_ref[d:d + 1, :] + w_lo[d]) for d in dirs]
        ld_hi = [ld[d].astype(bf16) for d in dirs]
        ld_lo = [(ld[d] - ld_hi[d].astype(f32)).astype(bf16) for d in dirs]
        cs = [_dot(tri[d], ld_hi[d]) + _dot(tri[d], ld_lo[d]) for d in dirs]
        yield
        a = [_sigmoid(a0_ref[d:d + 1, :] + a_lo[d]) for d in dirs]
        k_dir = [k[d] * (1.0 + (a[d] - 1.0) * ka_ref[...]) for d in dirs]
        sums = seg_sum(jnp.concatenate(
            [kk[d] * kk[d] for d in dirs] + [r[d] * k_dir[d] * rk_ref[...] for d in dirs], axis=0))
        kk = [kk[d] * lax.rsqrt(jnp.maximum(sums[d * C:(d + 1) * C], 1e-24)) for d in dirs]
        bonus_s = [sums[(2 + d) * C:(3 + d) * C] for d in dirs]
        yield
        e_pos = [jnp.exp(cs[d]) for d in dirs]
        e_neg = [jnp.exp(-cs[d]) for d in dirs]
        e_prev = [jnp.exp(cs[d] - ld[d]) for d in dirs]
        p_tot = [e_pos[0][C - 1:, :], e_pos[1][:1, :]]
        for d in dirs:
            a_t = (-kk[d] * e_prev[d]).astype(bf16)
            b_t = kk[d] * a[d] * e_neg[d]
            k_t = k_dir[d] * e_neg[d]
            b_hat = b_t * p_tot[d]
            k_hat = k_t * p_tot[d]
            v_b = v[d].astype(bf16)
            opnd_scr[slot, d, _OP_R] = (r[d] * e_pos[d]).astype(bf16)
            opnd_scr[slot, d, _OP_A] = a_t
            opnd_scr[slot, d, _OP_V_H0] = v_b * keep_h0
            opnd_scr[slot, d, _OP_V_H1] = v_b * keep_h1
            opnd_scr[slot, d, _OP_V] = v_b
            bonus_scr[d, pl.ds(t0[d], C), :] = bonus_s[d] * v[d]
            for p, sl in enumerate(pair_lanes):
                n = d * N_PAIR + p
                b_p, k_p = b_t[:, sl], k_t[:, sl]
                y_all = jnp.concatenate([jnp.where(head0, b_p, 0.0), jnp.where(head0, 0.0, b_p),
                                         jnp.where(head0, k_p, 0.0), jnp.where(head0, 0.0, k_p)], axis=0)
                yt_scr[slot, n] = y_all.T.astype(bf16)
                bkt_scr[slot, n] = jnp.concatenate([b_hat[:, sl], k_hat[:, sl]], axis=0).T.astype(bf16)
                pcol_scr[slot, n] = jnp.broadcast_to(p_tot[d][:, sl], (LANES, LANES)).T
        yield

    def per_chain(fn):
        return [fn(n, d, pair_lanes[p]) for n, (d, p) in enumerate(chains)]

    def rows_of(slot, d, sl, *ops):
        parts = [opnd_scr[slot, d, j, :, sl] for j in ops]
        return parts[0] if len(parts) == 1 else jnp.concatenate(parts, axis=0)

    def solve(slot):
        s = per_chain(lambda n, d, sl: _dot(rows_of(slot, d, sl, _OP_A, _OP_R), yt_scr[slot, n]))
        yield
        a_ab = per_chain(lambda n, d, sl: jnp.where(strict[d], s[n][:C, :2 * C], 0.0))
        a_xk = per_chain(lambda n, d, sl: jnp.concatenate(
            [jnp.where(strict[d], s[n][:C, 2 * C:], 0.0), jnp.where(incl[d], s[n][C:, 2 * C:], 0.0)],
            axis=0).astype(bf16))
        a_rb = per_chain(lambda n, d, sl: jnp.where(incl[d], s[n][C:, :2 * C], 0.0).astype(bf16))
        xv = per_chain(lambda n, d, sl: _dot(a_xk[n], rows_of(slot, d, sl, _OP_V_H0, _OP_V_H1)))
        a_pow = per_chain(lambda n, d, sl: _dot(a_ab[n].astype(bf16), blockdiag_of(a_ab[n])))
        t_inv = per_chain(lambda n, d, sl: eye_pair + a_ab[n])
        yield
        n_sq = 2
        while n_sq < C:
            last = 2 * n_sq >= C
            lhs = per_chain(lambda n, d, sl: (t_inv[n] if last else jnp.concatenate(
                [t_inv[n], a_pow[n]], axis=0)).astype(bf16))
            prod = per_chain(lambda n, d, sl: _dot(lhs[n], blockdiag_of(a_pow[n])))
            t_inv = per_chain(lambda n, d, sl: t_inv[n] + prod[n][:C])
            if not last:
                a_pow = per_chain(lambda n, d, sl: prod[n][C:])
            n_sq *= 2
            if last:
                for n in range(len(chains)):
                    sol_t[n] = t_inv[n].astype(bf16)
                    sol_arb[n] = a_rb[n]
                    sol_xv[n] = xv[n]
            yield

    def carry_state(slot, t0):
        h = [h_scr[n] for n in range(len(chains))]
        ah = per_chain(lambda n, d, sl: _dot(rows_of(slot, d, sl, _OP_A, _OP_R), h[n].astype(bf16)))
        yield
        u = per_chain(lambda n, d, sl: _dot(sol_t[n], stack(ah[n][:C] + sol_xv[n, :C, :])))
        yield
        o = per_chain(lambda n, d, sl: ah[n][C:] + _dot(sol_arb[n], stack(u[n])) + sol_xv[n, C:, :])
        upd = per_chain(lambda n, d, sl: _dot(
            bkt_scr[slot, n], jnp.concatenate([u[n].astype(bf16), rows_of(slot, d, sl, _OP_V)], axis=0)))
        for n in range(len(chains)):
            h_scr[n] = h[n] * pcol_scr[slot, n] + jnp.where(blockdiag, upd[n], 0.0)
        for d in dirs:
            o_scr[d, pl.ds(t0[d], C), :] = jnp.concatenate(o[d * N_PAIR:(d + 1) * N_PAIR], axis=1)
        yield

    def trip_chunks(j):
        return (jnp.minimum(j, n_chunks - 1), jnp.maximum(n_chunks - 1 - j, 0))

    def finish(done_trip):
        both = [pl.ds(pl.multiple_of(c * C, C), C) for c in (done_trip, n_chunks - 1 - done_trip)]
        o = jnp.concatenate([o_scr[0, rows, :] + o_scr[1, rows, :] for rows in both], axis=0)
        mean = seg_sum(o) * (1.0 / HEAD)
        yield
        oc = o - mean
        var = seg_sum(oc * oc) * (1.0 / HEAD)
        yield
        y = oc * lax.rsqrt(var + GN_EPS) * gnw_ref[...] + gnb_ref[...]
        for j, rows in enumerate(both):
            gate = ga_ref[rows, :].astype(f32)
            y_j = y[j * C:(j + 1) * C] + bonus_scr[0, rows, :] + bonus_scr[1, rows, :]
            out_ref[rows, :] = (y_j * (gate * _sigmoid(gate))).astype(bf16)
        yield

    def interleave(state=None, ahead=None, prep=None, fin=None):
        stages = {"s": state, "a": ahead, "p": prep, "f": fin}
        for c in "sapfasapfasapfapap":
            if stages[c] is not None:
                next(stages[c])

    def trip(i, with_solve=True, with_prep=True, with_finish=False):
        cur = i % 2
        chunk = trip_chunks(i)
        interleave(state=carry_state(cur, [pl.multiple_of(chunk[d] * C, C) for d in dirs]),
                   ahead=solve(1 - cur) if with_solve else None,
                   prep=prepare(trip_chunks(i + 2), cur) if with_prep else None,
                   fin=finish(i - 1) if with_finish else None)

    def loop(lo, hi, body=trip, **kw):
        def step(i, carry):
            body(i, **kw)
            return carry
        lax.fori_loop(lo, hi, step, 0)

    first_done = n_chunks // 2
    h_scr[...] = jnp.zeros_like(h_scr)
    interleave(prep=prepare(trip_chunks(0), 0))
    interleave(ahead=solve(0), prep=prepare(trip_chunks(1), 1))
    loop(0, first_done + 1)
    loop(first_done + 1, n_chunks - 2, with_finish=True)
    loop(n_chunks - 2, n_chunks - 1, with_prep=False, with_finish=True)
    loop(n_chunks - 1, n_chunks, with_solve=False, with_prep=False, with_finish=True)
    loop(n_chunks - 1, n_chunks, body=lambda i: interleave(fin=finish(i)))


def _rwkv(za, ga, layer_params, layer):
    B, T, _ = za.shape
    seq = lambda width: pl.BlockSpec((None, T, width), lambda b: (b, 0, 0))
    return pl.pallas_call(
        functools.partial(_rwkv_kernel, seq_len=T),
        grid=(B,),
        in_specs=[seq(A_SHIFT), seq(D_A)] + [_layer_spec(p, layer) for p in layer_params],
        out_specs=seq(D_A),
        out_shape=jax.ShapeDtypeStruct((B, T, D_A), bf16),
        scratch_shapes=[
            pltpu.VMEM((2 * N_PAIR, LANES, LANES), f32),
            pltpu.VMEM((2, T, D_A), f32),
            pltpu.VMEM((2, T, D_A), f32),
            pltpu.VMEM((2, 2, N_OPND, CHUNK, D_A), bf16),
            pltpu.VMEM((2, 2 * N_PAIR, LANES, 4 * CHUNK), bf16),
            pltpu.VMEM((2, 2 * N_PAIR, LANES, 2 * CHUNK), bf16),
            pltpu.VMEM((2, 2 * N_PAIR, LANES, LANES), f32),
            pltpu.VMEM((2 * N_PAIR, CHUNK, 2 * CHUNK), bf16),
            pltpu.VMEM((2 * N_PAIR, CHUNK, 2 * CHUNK), bf16),
            pltpu.VMEM((2 * N_PAIR, 2 * CHUNK, LANES), f32),
        ],
        compiler_params=pltpu.CompilerParams(
            dimension_semantics=("arbitrary",), vmem_limit_bytes=VMEM_LIMIT),
        name="rwkv7",
    )(za, ga, *layer_params)


def _natten_bias_table(rpb, n_rows):
    qc = np.arange(GRID_W)
    kc = np.arange(GRID_W)
    cs = np.clip(qc - WIN_C // 2, 0, GRID_W - WIN_C)
    col_mask = (kc[None, :] >= cs[:, None]) & (kc[None, :] < cs[:, None] + WIN_C)
    dc = np.clip(kc[None, :] - qc[:, None] + WIN_C - 1, 0, 2 * WIN_C - 2)
    tbl = jnp.where(col_mask[None, None], rpb.astype(f32)[:, :, dc], -1e30)
    n_heads = rpb.shape[0]
    first = _natten_window_offsets(n_rows)
    variants = []
    for a in range(KEY_ROWS):
        i = np.arange(KEY_ROWS)
        valid = (i >= first[a]) & (i < first[a] + MAX_KR)
        dr = np.clip(i - a + MAX_KR - 1, 0, 2 * MAX_KR - 2)
        t = jnp.where(valid[None, :, None, None], tbl[:, dr], -1e30)
        variants.append(t.transpose(0, 2, 1, 3).reshape(n_heads // 2, 2 * GRID_W, KEY_ROWS * GRID_W))
    return jnp.stack(variants, axis=0)


def _natten_step_rows(j, n_rows):
    r0 = Q_ROWS * j
    rs0 = jnp.clip(r0 - MAX_KR // 2, 0, n_rows - MAX_KR)
    return r0, jnp.minimum(rs0, n_rows - KEY_ROWS)


def _natten_window_offsets(n_rows):
    first = {}
    for j in range(n_rows // Q_ROWS):
        r0 = Q_ROWS * j
        st = min(int(np.clip(r0 - MAX_KR // 2, 0, n_rows - MAX_KR)), n_rows - KEY_ROWS)
        for r in range(r0, r0 + Q_ROWS):
            rs = int(np.clip(r - MAX_KR // 2, 0, n_rows - MAX_KR))
            assert 0 <= rs - st and rs - st + MAX_KR <= KEY_ROWS and 0 <= r - st < KEY_ROWS
            assert first.setdefault(r - st, rs - st) == rs - st
    return [first.get(a, 0) for a in range(KEY_ROWS)]


def _natten_kernel(qkv_ref, gb_ref, tbl_ref, out_ref, s_scr, *, seq_len):
    n_rows = seq_len // GRID_W
    n_steps = n_rows // Q_ROWS
    n_q = Q_ROWS * GRID_W
    n_keys = KEY_ROWS * GRID_W
    lane = lax.broadcasted_iota(jnp.int32, (1, LANES), 1)
    head0 = lane < HEAD
    pairs = range(N_PAIR)
    lanes = [slice(p * LANES, (p + 1) * LANES) for p in pairs]

    def stacked(x):
        zero = jnp.zeros_like(x[:GRID_W])
        parts = []
        for t in range(Q_ROWS):
            xt = x[t * GRID_W:(t + 1) * GRID_W]
            parts += [jnp.where(head0, xt, zero), jnp.where(head0, zero, xt)]
        return jnp.concatenate(parts, axis=0)

    def rows_of_step(j):
        r0, st = _natten_step_rows(j, n_rows)
        return (r0 - st, pl.ds(pl.multiple_of(r0 * GRID_W, n_q), n_q),
                pl.ds(pl.multiple_of(st * GRID_W, GRID_W), n_keys))

    def scores(j):
        a0, q_rows, k_rows = rows_of_step(j)
        for p in pairs:
            q2 = stacked(qkv_ref[q_rows, lanes[p]] * (HEAD ** -0.5))
            bias = jnp.concatenate([tbl_ref[a0 + t, p] for t in range(Q_ROWS)], axis=0)
            s_scr[p] = _dot_nt(q2, qkv_ref[k_rows, D_B + p * LANES:D_B + (p + 1) * LANES]) + bias

    def step(j, carry):
        _, q_rows, k_rows = rows_of_step(j)
        e = [jnp.exp(s_scr[p] - jnp.max(s_scr[p], axis=-1, keepdims=True)) for p in pairs]
        scores(jnp.minimum(j + 1, n_steps - 1))
        inv_den = [1.0 / jnp.sum(x, axis=-1, keepdims=True) for x in e]
        o = [_dot(e[p].astype(bf16), qkv_ref[k_rows, 2 * D_B + p * LANES:2 * D_B + (p + 1) * LANES]) * inv_den[p]
             for p in pairs]
        for p in pairs:
            gate = gb_ref[q_rows, lanes[p]].astype(f32)
            pair_out = jnp.concatenate(
                [jnp.where(head0, o[p][2 * t * GRID_W:(2 * t + 1) * GRID_W],
                           o[p][(2 * t + 1) * GRID_W:(2 * t + 2) * GRID_W]) for t in range(Q_ROWS)], axis=0)
            out_ref[q_rows, lanes[p]] = (pair_out * (gate * _sigmoid(gate))).astype(bf16)
        return carry

    scores(jnp.int32(0))
    lax.fori_loop(0, n_steps, step, 0)


def _natten(qkv, gb, tbl, layer):
    B, T, _ = qkv.shape
    seq = lambda width: pl.BlockSpec((None, T, width), lambda b: (b, 0, 0))
    return pl.pallas_call(
        functools.partial(_natten_kernel, seq_len=T),
        grid=(B,),
        in_specs=[seq(3 * D_B), seq(D_B), _layer_spec(tbl, layer, pipeline_mode=pl.Buffered(1))],
        out_specs=seq(D_B),
        out_shape=jax.ShapeDtypeStruct((B, T, D_B), bf16),
        scratch_shapes=[pltpu.VMEM((N_PAIR, Q_ROWS * 2 * GRID_W, KEY_ROWS * GRID_W), f32)],
        compiler_params=pltpu.CompilerParams(
            dimension_semantics=("arbitrary",), vmem_limit_bytes=VMEM_LIMIT),
        name="natten",
    )(qkv, gb, tbl)


def _merge_kernel(x_ref, ya_ref, yb_ref, m_ref, wpa_ref, wpb_ref, wout_ref, fg_ref, out_ref, *, final_norm):
    m = m_ref[...].astype(f32)
    merged = (_sigmoid(m[:, :D_MODEL]) * _dot(ya_ref[...], wpa_ref[...])
              + _sigmoid(m[:, D_MODEL:]) * _dot(yb_ref[...], wpb_ref[...]))
    y = x_ref[...] + _dot(merged.astype(bf16), wout_ref[...])
    if final_norm:
        y = _rms_norm(y, fg_ref[...])
    out_ref[...] = y


def _merge(x2d, ya, yb, m, w_pa, w_pb, w_out, final_g, layer, final_norm):
    n_tok = x2d.shape[0]
    tm = TOKEN_TILE
    const = lambda i: (0, 0)
    row = lambda i: (i, 0)
    return pl.pallas_call(
        functools.partial(_merge_kernel, final_norm=final_norm),
        grid=(n_tok // tm,),
        in_specs=[
            pl.BlockSpec((tm, D_MODEL), row), pl.BlockSpec((tm, D_A), row), pl.BlockSpec((tm, D_B), row),
            pl.BlockSpec((tm, 2 * D_MODEL), row),
            _layer_spec(w_pa, layer), _layer_spec(w_pb, layer), _layer_spec(w_out, layer),
            pl.BlockSpec((1, D_MODEL), const),
        ],
        out_specs=pl.BlockSpec((tm, D_MODEL), row),
        out_shape=jax.ShapeDtypeStruct((n_tok, D_MODEL), f32),
        compiler_params=pltpu.CompilerParams(
            dimension_semantics=("arbitrary",), vmem_limit_bytes=VMEM_LIMIT),
        name="merge",
    )(x2d, ya, yb, m, w_pa, w_pb, w_out, final_g)


def _pad_lora(w):
    z = jnp.zeros_like(w[:, 0])
    return jnp.stack([jnp.concatenate([w[:, 0], z], axis=1), jnp.concatenate([z, w[:, 1]], axis=1)],
                     axis=1).astype(bf16)


def _prepare_params(norm_g, w_in, shift_mu, w0, w2, a0, a2, k_k, k_a, r_k, gn_w, gn_b, rpb, w_pa, w_pb, w_out,
                    final_g, n_rows):
    depth = norm_g.shape[0]
    rows = lambda a: a.reshape(depth, 1, -1).astype(f32)
    return dict(
        norm_g=rows(norm_g), w_in=w_in.astype(bf16),
        rwkv=(rows(shift_mu), w0.astype(f32), _pad_lora(w2), a0.astype(f32), _pad_lora(a2), rows(k_k),
              rows(k_a), rows(r_k), rows(gn_w), rows(gn_b)),
        tbl=jax.vmap(lambda r: _natten_bias_table(r, n_rows))(rpb),
        w_pa=w_pa.astype(bf16), w_pb=w_pb.astype(bf16), w_out=w_out.astype(bf16),
        final_g=final_g.reshape(1, -1).astype(f32), depth=depth)


def _trunk(x, p):
    B, T, _ = x.shape
    x2d = x.reshape(B * T, D_MODEL)
    for l in range(p["depth"]):
        za, ga, qkv, gb, m = _inproj(x2d, p["norm_g"], p["w_in"], l)
        ya = _rwkv(za.reshape(B, T, A_SHIFT), ga.reshape(B, T, D_A), p["rwkv"], l)
        yb = _natten(qkv.reshape(B, T, 3 * D_B), gb.reshape(B, T, D_B), p["tbl"], l)
        x2d = _merge(x2d, ya.reshape(B * T, D_A), yb.reshape(B * T, D_B), m, p["w_pa"], p["w_pb"], p["w_out"],
                     p["final_g"], l, l == p["depth"] - 1)
    return x2d.reshape(B, T, D_MODEL)


def kernel(x_prompt, x_sample, norm_g, w_in, shift_mu, w0, w2, a0, a2, k_k, k_a, r_k, gn_w, gn_b, rpb, w_pa,
           w_pb, w_out, final_g):
    assert x_prompt.shape[1] == x_sample.shape[1]
    p = _prepare_params(norm_g, w_in, shift_mu, w0, w2, a0, a2, k_k, k_a, r_k, gn_w, gn_b, rpb, w_pa, w_pb,
                        w_out, final_g, x_prompt.shape[1] // GRID_W)
    return (_trunk(x_prompt, p), _trunk(x_sample, p))
```

```python
import functools
import math

import numpy as np
import jax
import jax.numpy as jnp
from jax import lax
from jax.experimental import pallas as pl
from jax.experimental.pallas import tpu as pltpu

f32 = jnp.float32
bf16 = jnp.bfloat16

D_MODEL = 1024
GRID_W = 64
D_A = 512
HEAD = 64
LORA = 64
GN_EPS = 6.4e-4
D_B = 512
MAX_KR = 8
WIN_C = 16
RMS_EPS = 1e-6
A_SHIFT = 3 * D_A + 4 * LORA
OFF_GA = A_SHIFT
OFF_QB = OFF_GA + D_A
OFF_GB = OFF_QB + 3 * D_B
OFF_MA = OFF_GB + D_B
D_IN = OFF_MA + 2 * D_MODEL

LANES = 128
N_PAIR = D_A // LANES
CHUNK = 64
FINISH_ROWS = 256
BF16_ROWS = 16
TOKEN_TILE = 512
VMEM_LIMIT = 56 * 1024 * 1024
LOG_DECAY_SCALE = -math.exp(-0.5)
Q_ROWS = 1
KEY_ROWS = MAX_KR

_NT = (((1,), (1,)), ((), ()))


def _sigmoid(x):
    return 1.0 / (1.0 + jnp.exp(-x))


def _dot(a, b):
    return jnp.dot(a, b, preferred_element_type=f32)


def _dot_nt(a, b):
    return lax.dot_general(a, b, _NT, preferred_element_type=f32)


def _rms_norm(x, g):
    return x * lax.rsqrt(jnp.mean(x * x, axis=-1, keepdims=True) + RMS_EPS) * g


_IN_SEGMENTS = ((0, OFF_GA), (OFF_GA, OFF_QB), (OFF_QB, OFF_GB), (OFF_GB, OFF_MA), (OFF_MA, D_IN))


def _inproj_kernel(x_ref, g_ref, w_ref, *out_refs):
    h = _rms_norm(x_ref[...], g_ref[...]).astype(bf16)
    for ref, (lo, hi) in zip(out_refs, _IN_SEGMENTS):
        ref[...] = _dot(h, w_ref[:, lo:hi]).astype(bf16)


def _layer_spec(array, layer, **kw):
    shape = array.shape[1:]
    return pl.BlockSpec((None,) + shape, lambda i: (layer,) + (0,) * len(shape), **kw)


def _inproj(x2d, g, w_bf16, layer):
    n_tok = x2d.shape[0]
    tm = TOKEN_TILE
    row = lambda i: (i, 0)
    return pl.pallas_call(
        _inproj_kernel,
        grid=(n_tok // tm,),
        in_specs=[
            pl.BlockSpec((tm, D_MODEL), row),
            _layer_spec(g, layer),
            _layer_spec(w_bf16, layer, pipeline_mode=pl.Buffered(1)),
        ],
        out_specs=[pl.BlockSpec((tm, hi - lo), row) for lo, hi in _IN_SEGMENTS],
        out_shape=[jax.ShapeDtypeStruct((n_tok, hi - lo), bf16) for lo, hi in _IN_SEGMENTS],
        compiler_params=pltpu.CompilerParams(
            dimension_semantics=("arbitrary",), vmem_limit_bytes=VMEM_LIMIT),
        name="inproj",
    )(x2d, g, w_bf16)


(_OP_R, _OP_A, _OP_V_H0, _OP_V_H1, _OP_V) = range(5)
N_OPND = 5


def _rwkv_kernel(za_ref, ga_ref, mu_ref, w0_ref, w2_ref, a0_ref, a2_ref, kk_ref, ka_ref, rk_ref,
                 gnw_ref, gnb_ref, out_ref, h_scr, o_scr, bonus_scr, opnd_scr, yt_scr, bkt_scr, pcol_scr,
                 sol_t, sol_arb, sol_xv, *, seq_len):
    C = CHUNK
    n_chunks = seq_len // C
    window = C + 2 * BF16_ROWS
    lane = lax.broadcasted_iota(jnp.int32, (1, LANES), 1)
    head0 = lane < HEAD
    lane_all = lax.broadcasted_iota(jnp.int32, (1, D_A), 1)
    first_head = ((lane_all // HEAD) % 2 == 0)
    keep_h0 = first_head.astype(bf16)
    keep_h1 = jnp.logical_not(first_head).astype(bf16)
    t_i = lax.broadcasted_iota(jnp.int32, (C, 2 * C), 0)
    s_i = lax.broadcasted_iota(jnp.int32, (C, 2 * C), 1) % C
    strict = (t_i > s_i, t_i < s_i)
    incl = (t_i >= s_i, t_i <= s_i)
    eye_pair = (t_i == s_i).astype(f32)
    r2 = lax.broadcasted_iota(jnp.int32, (2 * C, 2 * C), 0)
    c2 = lax.broadcasted_iota(jnp.int32, (2 * C, 2 * C), 1)
    blockdiag = (r2 // C) == (c2 // C)
    seg_ones = blockdiag.astype(bf16)
    tt = lax.broadcasted_iota(jnp.int32, (C, C), 0)
    ss = lax.broadcasted_iota(jnp.int32, (C, C), 1)
    tri = ((tt >= ss).astype(bf16), (tt <= ss).astype(bf16))
    w_row = lax.broadcasted_iota(jnp.int32, (C, window), 0)
    w_col = lax.broadcasted_iota(jnp.int32, (C, window), 1)

    def stack(x):
        return jnp.concatenate([jnp.where(head0, x, 0.0), jnp.where(head0, 0.0, x)], axis=0).astype(bf16)

    def blockdiag_of(x):
        return jnp.where(blockdiag, jnp.concatenate([x, x], axis=0), 0.0).astype(bf16)

    def seg_sum(x):
        return jnp.concatenate(
            [_dot(x[:, p * LANES:(p + 1) * LANES].astype(bf16), seg_ones) for p in range(N_PAIR)], axis=1)

    def shifted_chunk(t0):
        start = pl.multiple_of(jnp.clip(t0 - BF16_ROWS, 0, seq_len - window), BF16_ROWS)
        off = t0 - start
        pick = jnp.logical_or(w_col == w_row + (off - 1), w_col == w_row + (off + 1))
        nb = _dot(pick.astype(bf16), za_ref[pl.ds(start, window), :])
        zc = za_ref[pl.ds(t0, C), :].astype(f32)
        return zc + mu_ref[...] * (0.5 * nb - zc)

    dirs = (0, 1)
    chains = [(d, p) for d in dirs for p in range(N_PAIR)]
    pair_lanes = [slice(p * LANES, (p + 1) * LANES) for p in range(N_PAIR)]

    def prepare(chunk, slot):
        t0 = [pl.multiple_of(chunk[d] * C, C) for d in dirs]
        z = [shifted_chunk(t0[d]) for d in dirs]
        r = [z[d][:, 0:D_A] for d in dirs]
        k = [z[d][:, D_A:2 * D_A] for d in dirs]
        v = [z[d][:, 2 * D_A:3 * D_A] for d in dirs]
        yield
        w_lo = [_dot(jnp.tanh(z[d][:, 3 * D_A:3 * D_A + 2 * LORA]).astype(bf16), w2_ref[d]) for d in dirs]
        a_lo = [_dot(z[d][:, 3 * D_A + 2 * LORA:A_SHIFT].astype(bf16), a2_ref[d]) for d in dirs]
        kk = [k[d] * kk_ref[...] for d in dirs]
        yield
        ld = [LOG_DECAY_SCALE * _sigmoid(w0_ref[d:d + 1, :] + w_lo[d]) for d in dirs]
        ld_hi = [ld[d].astype(bf16) for d in dirs]
        ld_lo = [(ld[d] - ld_hi[d].astype(f32)).astype(bf16) for d in dirs]
        cs = [_dot(tri[d], ld_hi[d]) + _dot(tri[d], ld_lo[d]) for d in dirs]
        yield
        a = [_sigmoid(a0_ref[d:d + 1, :] + a_lo[d]) for d in dirs]
        k_dir = [k[d] * (1.0 + (a[d] - 1.0) * ka_ref[...]) for d in dirs]
        sums = seg_sum(jnp.concatenate(
            [kk[d] * kk[d] for d in dirs] + [r[d] * k_dir[d] * rk_ref[...] for d in dirs], axis=0))
        kk = [kk[d] * lax.rsqrt(jnp.maximum(sums[d * C:(d + 1) * C], 1e-24)) for d in dirs]
        bonus_s = [sums[(2 + d) * C:(3 + d) * C] for d in dirs]
        yield
        e_pos = [jnp.exp(cs[d]) for d in dirs]
        e_neg = [jnp.exp(-cs[d]) for d in dirs]
        e_prev = [jnp.exp(cs[d] - ld[d]) for d in dirs]
        p_tot = [e_pos[0][C - 1:, :], e_pos[1][:1, :]]
        for d in dirs:
            a_t = (-kk[d] * e_prev[d]).astype(bf16)
            b_t = kk[d] * a[d] * e_neg[d]
            k_t = k_dir[d] * e_neg[d]
            b_hat = b_t * p_tot[d]
            k_hat = k_t * p_tot[d]
            v_b = v[d].astype(bf16)
            opnd_scr[slot, d, _OP_R] = (r[d] * e_pos[d]).astype(bf16)
            opnd_scr[slot, d, _OP_A] = a_t
            opnd_scr[slot, d, _OP_V_H0] = v_b * keep_h0
            opnd_scr[slot, d, _OP_V_H1] = v_b * keep_h1
            opnd_scr[slot, d, _OP_V] = v_b
            bonus_scr[d, pl.ds(t0[d], C), :] = bonus_s[d] * v[d]
            for p, sl in enumerate(pair_lanes):
                n = d * N_PAIR + p
                b_p, k_p = b_t[:, sl], k_t[:, sl]
                y_all = jnp.concatenate([jnp.where(head0, b_p, 0.0), jnp.where(head0, 0.0, b_p),
                                         jnp.where(head0, k_p, 0.0), jnp.where(head0, 0.0, k_p)], axis=0)
                yt_scr[slot, n] = y_all.T.astype(bf16)
                bkt_scr[slot, n] = jnp.concatenate([b_hat[:, sl], k_hat[:, sl]], axis=0).T.astype(bf16)
                pcol_scr[slot, n] = jnp.broadcast_to(p_tot[d][:, sl], (LANES, LANES)).T
        yield

    def per_chain(fn):
        return [fn(n, d, pair_lanes[p]) for n, (d, p) in enumerate(chains)]

    def rows_of(slot, d, sl, *ops):
        parts = [opnd_scr[slot, d, j, :, sl] for j in ops]
        return parts[0] if len(parts) == 1 else jnp.concatenate(parts, axis=0)

    def solve(slot):
        s = per_chain(lambda n, d, sl: _dot(rows_of(slot, d, sl, _OP_A, _OP_R), yt_scr[slot, n]))
        yield
        a_ab = per_chain(lambda n, d, sl: jnp.where(strict[d], s[n][:C, :2 * C], 0.0))
        a_xk = per_chain(lambda n, d, sl: jnp.concatenate(
            [jnp.where(strict[d], s[n][:C, 2 * C:], 0.0), jnp.where(incl[d], s[n][C:, 2 * C:], 0.0)],
            axis=0).astype(bf16))
        a_rb = per_chain(lambda n, d, sl: jnp.where(incl[d], s[n][C:, :2 * C], 0.0).astype(bf16))
        xv = per_chain(lambda n, d, sl: _dot(a_xk[n], rows_of(slot, d, sl, _OP_V_H0, _OP_V_H1)))
        a_pow = per_chain(lambda n, d, sl: _dot(a_ab[n].astype(bf16), blockdiag_of(a_ab[n])))
        t_inv = per_chain(lambda n, d, sl: eye_pair + a_ab[n])
        yield
        n_sq = 2
        while n_sq < C:
            last = 2 * n_sq >= C
            lhs = per_chain(lambda n, d, sl: (t_inv[n] if last else jnp.concatenate(
                [t_inv[n], a_pow[n]], axis=0)).astype(bf16))
            prod = per_chain(lambda n, d, sl: _dot(lhs[n], blockdiag_of(a_pow[n])))
            t_inv = per_chain(lambda n, d, sl: t_inv[n] + prod[n][:C])
            if not last:
                a_pow = per_chain(lambda n, d, sl: prod[n][C:])
            n_sq *= 2
            if last:
                for n in range(len(chains)):
                    sol_t[n] = t_inv[n].astype(bf16)
                    sol_arb[n] = a_rb[n]
                    sol_xv[n] = xv[n]
            yield

    def carry_state(slot, t0):
        h = [h_scr[n] for n in range(len(chains))]
        ah = per_chain(lambda n, d, sl: _dot(rows_of(slot, d, sl, _OP_A, _OP_R), h[n].astype(bf16)))
        yield
        u = per_chain(lambda n, d, sl: _dot(sol_t[n], stack(ah[n][:C] + sol_xv[n, :C, :])))
        yield
        o = per_chain(lambda n, d, sl: ah[n][C:] + _dot(sol_arb[n], stack(u[n])) + sol_xv[n, C:, :])
        upd = per_chain(lambda n, d, sl: _dot(
            bkt_scr[slot, n], jnp.concatenate([u[n].astype(bf16), rows_of(slot, d, sl, _OP_V)], axis=0)))
        for n in range(len(chains)):
            h_scr[n] = h[n] * pcol_scr[slot, n] + jnp.where(blockdiag, upd[n], 0.0)
        for d in dirs:
            o_scr[d, pl.ds(t0[d], C), :] = jnp.concatenate(o[d * N_PAIR:(d + 1) * N_PAIR], axis=1)
        yield

    def trip_chunks(j):
        return (jnp.minimum(j, n_chunks - 1), jnp.maximum(n_chunks - 1 - j, 0))

    def finish(done_trip):
        both = [pl.ds(pl.multiple_of(c * C, C), C) for c in (done_trip, n_chunks - 1 - done_trip)]
        o = jnp.concatenate([o_scr[0, rows, :] + o_scr[1, rows, :] for rows in both], axis=0)
        mean = seg_sum(o) * (1.0 / HEAD)
        yield
        oc = o - mean
        var = seg_sum(oc * oc) * (1.0 / HEAD)
        yield
        y = oc * lax.rsqrt(var + GN_EPS) * gnw_ref[...] + gnb_ref[...]
        for j, rows in enumerate(both):
            gate = ga_ref[rows, :].astype(f32)
            y_j = y[j * C:(j + 1) * C] + bonus_scr[0, rows, :] + bonus_scr[1, rows, :]
            out_ref[rows, :] = (y_j * (gate * _sigmoid(gate))).astype(bf16)
        yield

    def interleave(state=None, ahead=None, prep=None, fin=None):
        stages = {"s": state, "a": ahead, "p": prep, "f": fin}
        for c in "sapfasapfasapfapap":
            if stages[c] is not None:
                next(stages[c])

    def trip(i, with_solve=True, with_prep=True, with_finish=False):
        cur = i % 2
        chunk = trip_chunks(i)
        interleave(state=carry_state(cur, [pl.multiple_of(chunk[d] * C, C) for d in dirs]),
                   ahead=solve(1 - cur) if with_solve else None,
                   prep=prepare(trip_chunks(i + 2), cur) if with_prep else None,
                   fin=finish(i - 1) if with_finish else None)

    def loop(lo, hi, body=trip, **kw):
        def step(i, carry):
            body(i, **kw)
            return carry
        lax.fori_loop(lo, hi, step, 0)

    first_done = n_chunks // 2
    h_scr[...] = jnp.zeros_like(h_scr)
    interleave(prep=prepare(trip_chunks(0), 0))
    interleave(ahead=solve(0), prep=prepare(trip_chunks(1), 1))
    loop(0, first_done + 1)
    loop(first_done + 1, n_chunks - 2, with_finish=True)
    loop(n_chunks - 2, n_chunks - 1, with_prep=False, with_finish=True)
    loop(n_chunks - 1, n_chunks, with_solve=False, with_prep=False, with_finish=True)
    loop(n_chunks - 1, n_chunks, body=lambda i: interleave(fin=finish(i)))


def _rwkv(za, ga, layer_params, layer):
    B, T, _ = za.shape
    seq = lambda width: pl.BlockSpec((None, T, width), lambda b: (b, 0, 0))
    return pl.pallas_call(
        functools.partial(_rwkv_kernel, seq_len=T),
        grid=(B,),
        in_specs=[seq(A_SHIFT), seq(D_A)] + [_layer_spec(p, layer) for p in layer_params],
        out_specs=seq(D_A),
        out_shape=jax.ShapeDtypeStruct((B, T, D_A), bf16),
        scratch_shapes=[
            pltpu.VMEM((2 * N_PAIR, LANES, LANES), f32),
            pltpu.VMEM((2, T, D_A), f32),
            pltpu.VMEM((2, T, D_A), f32),
            pltpu.VMEM((2, 2, N_OPND, CHUNK, D_A), bf16),
            pltpu.VMEM((2, 2 * N_PAIR, LANES, 4 * CHUNK), bf16),
            pltpu.VMEM((2, 2 * N_PAIR, LANES, 2 * CHUNK), bf16),
            pltpu.VMEM((2, 2 * N_PAIR, LANES, LANES), f32),
            pltpu.VMEM((2 * N_PAIR, CHUNK, 2 * CHUNK), bf16),
            pltpu.VMEM((2 * N_PAIR, CHUNK, 2 * CHUNK), bf16),
            pltpu.VMEM((2 * N_PAIR, 2 * CHUNK, LANES), f32),
        ],
        compiler_params=pltpu.CompilerParams(
            dimension_semantics=("arbitrary",), vmem_limit_bytes=VMEM_LIMIT),
        name="rwkv7",
    )(za, ga, *layer_params)


def _natten_bias_table(rpb, n_rows):
    qc = np.arange(GRID_W)
    kc = np.arange(GRID_W)
    cs = np.clip(qc - WIN_C // 2, 0, GRID_W - WIN_C)
    col_mask = (kc[None, :] >= cs[:, None]) & (kc[None, :] < cs[:, None] + WIN_C)
    dc = np.clip(kc[None, :] - qc[:, None] + WIN_C - 1, 0, 2 * WIN_C - 2)
    tbl = jnp.where(col_mask[None, None], rpb.astype(f32)[:, :, dc], -1e30)
    n_heads = rpb.shape[0]
    first = _natten_window_offsets(n_rows)
    variants = []
    for a in range(KEY_ROWS):
        i = np.arange(KEY_ROWS)
        valid = (i >= first[a]) & (i < first[a] + MAX_KR)
        dr = np.clip(i - a + MAX_KR - 1, 0, 2 * MAX_KR - 2)
        t = jnp.where(valid[None, :, None, None], tbl[:, dr], -1e30)
        variants.append(t.transpose(0, 2, 1, 3).reshape(n_heads // 2, 2 * GRID_W, KEY_ROWS * GRID_W))
    return jnp.stack(variants, axis=0)


def _natten_step_rows(j, n_rows):
    r0 = Q_ROWS * j
    rs0 = jnp.clip(r0 - MAX_KR // 2, 0, n_rows - MAX_KR)
    return r0, jnp.minimum(rs0, n_rows - KEY_ROWS)


def _natten_window_offsets(n_rows):
    first = {}
    for j in range(n_rows // Q_ROWS):
        r0 = Q_ROWS * j
        st = min(int(np.clip(r0 - MAX_KR // 2, 0, n_rows - MAX_KR)), n_rows - KEY_ROWS)
        for r in range(r0, r0 + Q_ROWS):
            rs = int(np.clip(r - MAX_KR // 2, 0, n_rows - MAX_KR))
            assert 0 <= rs - st and rs - st + MAX_KR <= KEY_ROWS and 0 <= r - st < KEY_ROWS
            assert first.setdefault(r - st, rs - st) == rs - st
    return [first.get(a, 0) for a in range(KEY_ROWS)]


def _natten_kernel(qkv_ref, gb_ref, tbl_ref, out_ref, s_scr, *, seq_len):
    n_rows = seq_len // GRID_W
    n_steps = n_rows // Q_ROWS
    n_q = Q_ROWS * GRID_W
    n_keys = KEY_ROWS * GRID_W
    lane = lax.broadcasted_iota(jnp.int32, (1, LANES), 1)
    head0 = lane < HEAD
    pairs = range(N_PAIR)
    lanes = [slice(p * LANES, (p + 1) * LANES) for p in pairs]

    def stacked(x):
        zero = jnp.zeros_like(x[:GRID_W])
        parts = []
        for t in range(Q_ROWS):
            xt = x[t * GRID_W:(t + 1) * GRID_W]
            parts += [jnp.where(head0, xt, zero), jnp.where(head0, zero, xt)]
        return jnp.concatenate(parts, axis=0)

    def rows_of_step(j):
        r0, st = _natten_step_rows(j, n_rows)
        return (r0 - st, pl.ds(pl.multiple_of(r0 * GRID_W, n_q), n_q),
                pl.ds(pl.multiple_of(st * GRID_W, GRID_W), n_keys))

    def scores(j):
        a0, q_rows, k_rows = rows_of_step(j)
        for p in pairs:
            q2 = stacked(qkv_ref[q_rows, lanes[p]] * (HEAD ** -0.5))
            bias = jnp.concatenate([tbl_ref[a0 + t, p] for t in range(Q_ROWS)], axis=0)
            s_scr[p] = _dot_nt(q2, qkv_ref[k_rows, D_B + p * LANES:D_B + (p + 1) * LANES]) + bias

    def step(j, carry):
        _, q_rows, k_rows = rows_of_step(j)
        e = [jnp.exp(s_scr[p] - jnp.max(s_scr[p], axis=-1, keepdims=True)) for p in pairs]
        scores(jnp.minimum(j + 1, n_steps - 1))
        inv_den = [1.0 / jnp.sum(x, axis=-1, keepdims=True) for x in e]
        o = [_dot(e[p].astype(bf16), qkv_ref[k_rows, 2 * D_B + p * LANES:2 * D_B + (p + 1) * LANES]) * inv_den[p]
             for p in pairs]
        for p in pairs:
            gate = gb_ref[q_rows, lanes[p]].astype(f32)
            pair_out = jnp.concatenate(
                [jnp.where(head0, o[p][2 * t * GRID_W:(2 * t + 1) * GRID_W],
                           o[p][(2 * t + 1) * GRID_W:(2 * t + 2) * GRID_W]) for t in range(Q_ROWS)], axis=0)
            out_ref[q_rows, lanes[p]] = (pair_out * (gate * _sigmoid(gate))).astype(bf16)
        return carry

    scores(jnp.int32(0))
    lax.fori_loop(0, n_steps, step, 0)


def _natten(qkv, gb, tbl, layer):
    B, T, _ = qkv.shape
    seq = lambda width: pl.BlockSpec((None, T, width), lambda b: (b, 0, 0))
    return pl.pallas_call(
        functools.partial(_natten_kernel, seq_len=T),
        grid=(B,),
        in_specs=[seq(3 * D_B), seq(D_B), _layer_spec(tbl, layer, pipeline_mode=pl.Buffered(1))],
        out_specs=seq(D_B),
        out_shape=jax.ShapeDtypeStruct((B, T, D_B), bf16),
        scratch_shapes=[pltpu.VMEM((N_PAIR, Q_ROWS * 2 * GRID_W, KEY_ROWS * GRID_W), f32)],
        compiler_params=pltpu.CompilerParams(
            dimension_semantics=("arbitrary",), vmem_limit_bytes=VMEM_LIMIT),
        name="natten",
    )(qkv, gb, tbl)


def _merge_kernel(x_ref, ya_ref, yb_ref, m_ref, wpa_ref, wpb_ref, wout_ref, fg_ref, out_ref, *, final_norm):
    m = m_ref[...].astype(f32)
    merged = (_sigmoid(m[:, :D_MODEL]) * _dot(ya_ref[...], wpa_ref[...])
              + _sigmoid(m[:, D_MODEL:]) * _dot(yb_ref[...], wpb_ref[...]))
    y = x_ref[...] + _dot(merged.astype(bf16), wout_ref[...])
    if final_norm:
        y = _rms_norm(y, fg_ref[...])
    out_ref[...] = y


def _merge_inproj_kernel(x_ref, ya_ref, yb_ref, m_ref, wpa_ref, wpb_ref, wout_ref, g_ref, win_ref, xout_ref,
                         *out_refs):
    m = m_ref[...].astype(f32)
    merged = (_sigmoid(m[:, :D_MODEL]) * _dot(ya_ref[...], wpa_ref[...])
              + _sigmoid(m[:, D_MODEL:]) * _dot(yb_ref[...], wpb_ref[...]))
    y = x_ref[...] + _dot(merged.astype(bf16), wout_ref[...])
    xout_ref[...] = y
    h = _rms_norm(y, g_ref[...]).astype(bf16)
    for ref, (lo, hi) in zip(out_refs, _IN_SEGMENTS):
        ref[...] = _dot(h, win_ref[:, lo:hi]).astype(bf16)


def _merge_inproj(x2d, ya, yb, m, w_pa, w_pb, w_out, norm_g, w_in, layer):
    n_tok = x2d.shape[0]
    tm = TOKEN_TILE
    row = lambda i: (i, 0)
    once = dict(pipeline_mode=pl.Buffered(1))
    return pl.pallas_call(
        _merge_inproj_kernel,
        grid=(n_tok // tm,),
        in_specs=[
            pl.BlockSpec((tm, D_MODEL), row), pl.BlockSpec((tm, D_A), row), pl.BlockSpec((tm, D_B), row),
            pl.BlockSpec((tm, 2 * D_MODEL), row),
            _layer_spec(w_pa, layer, **once), _layer_spec(w_pb, layer, **once), _layer_spec(w_out, layer, **once),
            _layer_spec(norm_g, layer + 1), _layer_spec(w_in, layer + 1, **once),
        ],
        out_specs=[pl.BlockSpec((tm, D_MODEL), row)] + [pl.BlockSpec((tm, hi - lo), row) for lo, hi in _IN_SEGMENTS],
        out_shape=[jax.ShapeDtypeStruct((n_tok, D_MODEL), f32)]
        + [jax.ShapeDtypeStruct((n_tok, hi - lo), bf16) for lo, hi in _IN_SEGMENTS],
        compiler_params=pltpu.CompilerParams(
            dimension_semantics=("arbitrary",), vmem_limit_bytes=VMEM_LIMIT),
        name="merge_inproj",
    )(x2d, ya, yb, m, w_pa, w_pb, w_out, norm_g, w_in)


def _merge(x2d, ya, yb, m, w_pa, w_pb, w_out, final_g, layer, final_norm):
    n_tok = x2d.shape[0]
    tm = TOKEN_TILE
    const = lambda i: (0, 0)
    row = lambda i: (i, 0)
    return pl.pallas_call(
        functools.partial(_merge_kernel, final_norm=final_norm),
        grid=(n_tok // tm,),
        in_specs=[
            pl.BlockSpec((tm, D_MODEL), row), pl.BlockSpec((tm, D_A), row), pl.BlockSpec((tm, D_B), row),
            pl.BlockSpec((tm, 2 * D_MODEL), row),
            _layer_spec(w_pa, layer), _layer_spec(w_pb, layer), _layer_spec(w_out, layer),
            pl.BlockSpec((1, D_MODEL), const),
        ],
        out_specs=pl.BlockSpec((tm, D_MODEL), row),
        out_shape=jax.ShapeDtypeStruct((n_tok, D_MODEL), f32),
        compiler_params=pltpu.CompilerParams(
            dimension_semantics=("arbitrary",), vmem_limit_bytes=VMEM_LIMIT),
        name="merge",
    )(x2d, ya, yb, m, w_pa, w_pb, w_out, final_g)


def _pad_lora(w):
    z = jnp.zeros_like(w[:, 0])
    return jnp.stack([jnp.concatenate([w[:, 0], z], axis=1), jnp.concatenate([z, w[:, 1]], axis=1)],
                     axis=1).astype(bf16)


def _prepare_params(norm_g, w_in, shift_mu, w0, w2, a0, a2, k_k, k_a, r_k, gn_w, gn_b, rpb, w_pa, w_pb, w_out,
                    final_g, n_rows):
    depth = norm_g.shape[0]
    rows = lambda a: a.reshape(depth, 1, -1).astype(f32)
    return dict(
        norm_g=rows(norm_g), w_in=w_in.astype(bf16),
        rwkv=(rows(shift_mu), w0.astype(f32), _pad_lora(w2), a0.astype(f32), _pad_lora(a2), rows(k_k),
              rows(k_a), rows(r_k), rows(gn_w), rows(gn_b)),
        tbl=jax.vmap(lambda r: _natten_bias_table(r, n_rows))(rpb),
        w_pa=w_pa.astype(bf16), w_pb=w_pb.astype(bf16), w_out=w_out.astype(bf16),
        final_g=final_g.reshape(1, -1).astype(f32), depth=depth)


def _trunk(x, p):
    B, T, _ = x.shape
    x2d = x.reshape(B * T, D_MODEL)
    last = p["depth"] - 1
    za, ga, qkv, gb, m = _inproj(x2d, p["norm_g"], p["w_in"], 0)
    for l in range(p["depth"]):
        ya = _rwkv(za.reshape(B, T, A_SHIFT), ga.reshape(B, T, D_A), p["rwkv"], l).reshape(B * T, D_A)
        yb = _natten(qkv.reshape(B, T, 3 * D_B), gb.reshape(B, T, D_B), p["tbl"], l).reshape(B * T, D_B)
        if l < last:
            x2d, za, ga, qkv, gb, m = _merge_inproj(x2d, ya, yb, m, p["w_pa"], p["w_pb"], p["w_out"],
                                                    p["norm_g"], p["w_in"], l)
        else:
            x2d = _merge(x2d, ya, yb, m, p["w_pa"], p["w_pb"], p["w_out"], p["final_g"], l, True)
    return x2d.reshape(B, T, D_MODEL)


def kernel(x_prompt, x_sample, norm_g, w_in, shift_mu, w0, w2, a0, a2, k_k, k_a, r_k, gn_w, gn_b, rpb, w_pa,
           w_pb, w_out, final_g):
    assert x_prompt.shape[1] == x_sample.shape[1]
    p = _prepare_params(norm_g, w_in, shift_mu, w0, w2, a0, a2, k_k, k_a, r_k, gn_w, gn_b, rpb, w_pa, w_pb,
                        w_out, final_g, x_prompt.shape[1] // GRID_W)
    return (_trunk(x_prompt, p), _trunk(x_sample, p))
```
